```python
import math
import jax, jax.numpy as jnp
from jax import lax
import numpy as np

D_MODEL = 1024
BATCH = 4
SEQ = 8192
DEPTH = 4

GRID_W = 64
CTX_LEN = 256
N_BRANCH = 4
BRANCH_W = D_MODEL // 2
CONV_A_K = 3
CONF_K = 31
DIFF_HEADS = 4
DIFF_D = D_MODEL // 16
DIFF_V = 2 * DIFF_D
NA_HEADS = 8
NA_D = D_MODEL // 16
NA_WIN_R = 8
NA_WIN_C = 16
D_FF = 4 * D_MODEL
Q_BLOCK = 128
ROPE_BASE = 10000.0
LN_EPS = 1e-5
DEEPNORM_ALPHA = (2 * DEPTH) ** 0.25
DEEPNORM_BETA = (8 * DEPTH) ** -0.25
NEG_INF = -1e30

IN_WIDTHS = (BRANCH_W, BRANCH_W, BRANCH_W,
             BRANCH_W, BRANCH_W,
             DIFF_HEADS * 2 * DIFF_D, DIFF_HEADS * 2 * DIFF_D, DIFF_HEADS * DIFF_V,
             NA_HEADS * NA_D, NA_HEADS * NA_D, NA_HEADS * NA_D,
             N_BRANCH * D_MODEL)
IN_OFFSETS = tuple(sum(IN_WIDTHS[:i]) for i in range(len(IN_WIDTHS)))
D_IN = sum(IN_WIDTHS)
I_AB, I_AC, I_AX, I_CA, I_CG, I_DQ, I_DK, I_DV, I_NQ, I_NK, I_NV, I_GATE = range(12)

kernel_name = 'hybrid_parallel_diffusion_trunk'


def layer_norm(x, g, b):
    xf = x.astype(jnp.float32)
    mu = jnp.mean(xf, axis=-1, keepdims=True)
    var = jnp.mean(jnp.square(xf - mu), axis=-1, keepdims=True)
    return ((xf - mu) * lax.rsqrt(var + LN_EPS)).astype(x.dtype) * g + b


def rms_norm(x, g):
    xf = x.astype(jnp.float32)
    return (xf * lax.rsqrt(jnp.mean(xf * xf, axis=-1, keepdims=True) + LN_EPS)).astype(x.dtype) * g


def dwconv(x, w, b):
    k = w.shape[0]
    y = lax.conv_general_dilated(x, w[:, None, :], window_strides=(1,), padding=((k // 2, k // 2),),
                                 dimension_numbers=('NWC', 'WIO', 'NWC'), feature_group_count=x.shape[-1])
    return y + b


def ada_mod(cond, w, b):
    return jnp.split(jax.nn.silu(cond) @ w + b, 6, axis=-1)


def in_cols(u, w, b, i):
    lo = IN_OFFSETS[i]
    hi = lo + IN_WIDTHS[i]
    return u @ w[:, lo:hi] + b[lo:hi]


def split_proj(proj):
    return jnp.split(proj, list(IN_OFFSETS[1:]), axis=-1)


def axial_rope_tables(n):
    t = jnp.arange(n)
    nf = DIFF_D // 4
    freqs = jnp.power(ROPE_BASE, -jnp.arange(nf, dtype=jnp.float32) / nf)
    pos = jnp.stack([t // GRID_W, t % GRID_W], axis=-1).astype(jnp.float32)
    ang = pos[:, :, None] * freqs
    return jnp.cos(ang), jnp.sin(ang)


def axial_rope(x, cos, sin):
    xs = x.reshape(x.shape[:-1] + (2, 2, DIFF_D // 4))
    x1, x2 = xs[..., 0, :], xs[..., 1, :]
    cs = cos[:, None, None].astype(x.dtype)
    sn = sin[:, None, None].astype(x.dtype)
    out = jnp.stack([x1 * cs - x2 * sn, x2 * cs + x1 * sn], axis=-2)
    return out.reshape(x.shape)


def local_branches(p, cw, cb, dw, db, lg, lb):
    y_a = p[I_AB] * dwconv(p[I_AC] * p[I_AX], cw, cb)
    glu = p[I_CA] * jax.nn.sigmoid(p[I_CG])
    y_b = jax.nn.silu(layer_norm(dwconv(glu, dw, db), lg, lb))
    return y_a, y_b


def diff_core(q, k, v, lam):
    s = jnp.einsum('bqhmd,bkhmd->bhmqk', q, k).astype(jnp.float32)
    p = jax.nn.softmax(s, axis=-1)
    pd = (p[:, :, 0] - lam * p[:, :, 1]).astype(v.dtype)
    return jnp.einsum('bhqk,bkhv->bqhv', pd, v)


def diff_latent(q, k, v, k_ctx, v_ctx, lam):
    B, S, H, M, d = q.shape
    k_all = jnp.concatenate([k_ctx, k], axis=1)
    v_all = jnp.concatenate([v_ctx, v], axis=1)
    qb = q.reshape(B, S // Q_BLOCK, Q_BLOCK, H, M, d).swapaxes(0, 1)
    o = lax.map(lambda qi: diff_core(qi, k_all, v_all, lam), qb)
    return o.swapaxes(0, 1).reshape(B, S, H, v.shape[-1])


def diff_post(o, g, lam_init):
    B, T = o.shape[:2]
    return (rms_norm(o, g) * (1.0 - lam_init)).reshape(B, T, -1)


def dense_attn(q, k, v):
    B, T = q.shape[:2]
    s = jnp.einsum('bqhd,bkhd->bhqk', q, k).astype(jnp.float32)
    p = jax.nn.softmax(s, axis=-1).astype(v.dtype)
    return jnp.einsum('bhqk,bkhd->bqhd', p, v).reshape(B, T, -1)


def na_latent(q, k, v, k_ctx, v_ctx, rpb):
    B, S, H, d = q.shape
    rows = S // GRID_W
    win_r = min(NA_WIN_R, rows)
    kg = k.reshape(B, rows, GRID_W, H, d)
    vg = v.reshape(B, rows, GRID_W, H, d)
    qg = q.reshape(B, rows, GRID_W, H, d).swapaxes(0, 1)
    cidx = jnp.arange(GRID_W)
    cstart = jnp.clip(cidx - NA_WIN_C // 2, 0, GRID_W - NA_WIN_C)
    col_ok = (cidx[None, :] >= cstart[:, None]) & (cidx[None, :] < cstart[:, None] + NA_WIN_C)
    dc_idx = jnp.clip(cidx[None, :] - cidx[:, None] + NA_WIN_C - 1, 0, 2 * NA_WIN_C - 2)
    rpb_c = rpb[:, :, dc_idx]
    n_lat = win_r * GRID_W

    def row_fn(args):
        qr, r = args
        r0 = jnp.clip(r - win_r // 2, 0, rows - win_r)
        kb = lax.dynamic_slice_in_dim(kg, r0, win_r, axis=1)
        vb = lax.dynamic_slice_in_dim(vg, r0, win_r, axis=1)
        dr_idx = r0 + jnp.arange(win_r) - r + NA_WIN_R - 1
        bias = rpb_c[:, dr_idx].transpose(0, 2, 1, 3)[None].astype(jnp.float32)
        s_lat = jnp.einsum('bqhd,bwkhd->bhqwk', qr, kb).astype(jnp.float32) + bias
        s_lat = jnp.where(col_ok[:, None, :], s_lat, NEG_INF)
        s_ctx = jnp.einsum('bqhd,bkhd->bhqk', qr, k_ctx).astype(jnp.float32)
        s = jnp.concatenate([s_lat.reshape(B, H, GRID_W, n_lat), s_ctx], axis=-1)
        p = jax.nn.softmax(s, axis=-1).astype(v.dtype)
        p_lat = p[..., :n_lat].reshape(B, H, GRID_W, win_r, GRID_W)
        return (jnp.einsum('bhqwk,bwkhd->bqhd', p_lat, vb)
                + jnp.einsum('bhqk,bkhd->bqhd', p[..., n_lat:], v_ctx))

    o = lax.map(row_fn, (qg, jnp.arange(rows)))
    return o.swapaxes(0, 1).reshape(B, S, H * d)


def merge_out(gates, ys, wb, bb, wo):
    gs = jnp.split(gates, N_BRANCH, axis=-1)
    merged = jax.nn.sigmoid(gs[0]) * (ys[0] @ wb[0] + bb[0])
    for i in range(1, N_BRANCH):
        merged = merged + jax.nn.sigmoid(gs[i]) * (ys[i] @ wb[i] + bb[i])
    return merged @ wo


def sq_relu_ffn(h, w1, w2):
    a = jax.nn.relu(h @ w1)
    return (a * a) @ w2


def setup_inputs(seed: int = 0) -> dict:
    key = jax.random.key(seed)
    ks = jax.random.split(key, 24)
    nrm = lambda k, shape, s: jax.random.normal(k, shape, jnp.float32) * s
    gate_offset = jnp.repeat(jnp.array([0.0, 0.0, 1.0, 0.0, 0.0, 1.0], jnp.float32), D_MODEL)
    return {
        'x': nrm(ks[0], (BATCH, SEQ, D_MODEL), 1.0),
        'c': nrm(ks[1], (BATCH, D_MODEL), 1.0),
        'ctx': nrm(ks[2], (BATCH, CTX_LEN, D_MODEL), 1.0),
        'c_ctx': nrm(ks[3], (D_MODEL,), 1.0),
        'w_ada': nrm(ks[4], (DEPTH, D_MODEL, 6 * D_MODEL), 0.3 * D_MODEL ** -0.5),
        'b_ada': nrm(ks[5], (DEPTH, 6 * D_MODEL), 0.02) + gate_offset,
        'w_in': nrm(ks[6], (DEPTH, D_MODEL, D_IN), D_MODEL ** -0.5),
        'b_in': nrm(ks[7], (DEPTH, D_IN), 0.02),
        'conv_a_w': nrm(ks[8], (DEPTH, CONV_A_K, BRANCH_W), CONV_A_K ** -0.5),
        'conv_a_b': nrm(ks[9], (DEPTH, BRANCH_W), 0.02),
        'conf_dw_w': nrm(ks[10], (DEPTH, CONF_K, BRANCH_W), CONF_K ** -0.5),
        'conf_dw_b': nrm(ks[11], (DEPTH, BRANCH_W), 0.02),
        'conf_ln_g': 1.0 + nrm(ks[12], (DEPTH, BRANCH_W), 0.02),
        'conf_ln_b': nrm(ks[13], (DEPTH, BRANCH_W), 0.02),
        'diff_lambda': nrm(ks[14], (DEPTH, 4, DIFF_D), 0.1),
        'diff_norm_g': 1.0 + nrm(ks[15], (DEPTH, DIFF_V), 0.02),
        'na_rpb': nrm(ks[16], (DEPTH, NA_HEADS, 2 * NA_WIN_R - 1, 2 * NA_WIN_C - 1), 0.02),
        'w_branch': nrm(ks[17], (DEPTH, N_BRANCH, BRANCH_W, D_MODEL), BRANCH_W ** -0.5),
        'b_branch': nrm(ks[18], (DEPTH, N_BRANCH, D_MODEL), 0.02),
        'w_o': nrm(ks[19], (DEPTH, D_MODEL, D_MODEL), DEEPNORM_BETA * D_MODEL ** -0.5),
        'ln_g': 1.0 + nrm(ks[20], (DEPTH, 2, D_MODEL), 0.02),
        'ln_b': nrm(ks[21], (DEPTH, 2, D_MODEL), 0.02),
        'w_ff1': nrm(ks[22], (DEPTH, D_MODEL, D_FF), D_MODEL ** -0.5),
        'w_ff2': nrm(ks[23], (DEPTH, D_FF, D_MODEL), DEEPNORM_BETA * D_FF ** -0.5),
    }


def reference(x, c, ctx, c_ctx, w_ada, b_ada, w_in, b_in, conv_a_w, conv_a_b, conf_dw_w, conf_dw_b,
              conf_ln_g, conf_ln_b, diff_lambda, diff_norm_g, na_rpb, w_branch, b_branch, w_o,
              ln_g, ln_b, w_ff1, w_ff2):
    B, S, _ = x.shape
    rope_cos, rope_sin = axial_rope_tables(S)
    xc = ctx
    for l in range(DEPTH):
        last = l == DEPTH - 1
        L = xc.shape[1]
        lam_init = 0.8 - 0.6 * math.exp(-0.3 * l)
        lq1, lk1, lq2, lk2 = diff_lambda[l].astype(jnp.float32)
        lam = jnp.exp(jnp.sum(lq1 * lk1)) - jnp.exp(jnp.sum(lq2 * lk2)) + lam_init
        sh1, sc1, g1, sh2, sc2, g2 = ada_mod(c[:, None, :], w_ada[l], b_ada[l])
        csh1, csc1, cg1, csh2, csc2, cg2 = ada_mod(c_ctx, w_ada[l], b_ada[l])

        u_c = xc * (1.0 + csc1) + csh1
        if last:
            dk_c, dv_c, nk_c, nv_c = (in_cols(u_c, w_in[l], b_in[l], i) for i in (I_DK, I_DV, I_NK, I_NV))
        else:
            pc = split_proj(u_c @ w_in[l] + b_in[l])
            dk_c, dv_c, nk_c, nv_c = pc[I_DK], pc[I_DV], pc[I_NK], pc[I_NV]
        k_dc = dk_c.reshape(B, L, DIFF_HEADS, 2, DIFF_D)
        v_dc = dv_c.reshape(B, L, DIFF_HEADS, DIFF_V)
        k_nc = nk_c.reshape(B, L, NA_HEADS, NA_D)
        v_nc = nv_c.reshape(B, L, NA_HEADS, NA_D)

        u = x * (1.0 + sc1) + sh1
        p = split_proj(u @ w_in[l] + b_in[l])
        y_a, y_b = local_branches(p, conv_a_w[l], conv_a_b[l], conf_dw_w[l], conf_dw_b[l],
                                  conf_ln_g[l], conf_ln_b[l])
        q_d = axial_rope(p[I_DQ].reshape(B, S, DIFF_HEADS, 2, DIFF_D), rope_cos, rope_sin) * DIFF_D ** -0.5
        k_d = axial_rope(p[I_DK].reshape(B, S, DIFF_HEADS, 2, DIFF_D), rope_cos, rope_sin)
        v_d = p[I_DV].reshape(B, S, DIFF_HEADS, DIFF_V)
        y_c = diff_post(diff_latent(q_d, k_d, v_d, k_dc, v_dc, lam), diff_norm_g[l], lam_init)
        q_n = p[I_NQ].reshape(B, S, NA_HEADS, NA_D) * NA_D ** -0.5
        k_n = p[I_NK].reshape(B, S, NA_HEADS, NA_D)
        v_n = p[I_NV].reshape(B, S, NA_HEADS, NA_D)
        y_d = na_latent(q_n, k_n, v_n, k_nc, v_nc, na_rpb[l])
        y = merge_out(p[I_GATE], (y_a, y_b, y_c, y_d), w_branch[l], b_branch[l], w_o[l])
        x = layer_norm(DEEPNORM_ALPHA * x + g1 * y, ln_g[l, 0], ln_b[l, 0])
        x = layer_norm(DEEPNORM_ALPHA * x + g2 * sq_relu_ffn(x * (1.0 + sc2) + sh2, w_ff1[l], w_ff2[l]),
                       ln_g[l, 1], ln_b[l, 1])

        if not last:
            y_ac, y_bc = local_branches(pc, conv_a_w[l], conv_a_b[l], conf_dw_w[l], conf_dw_b[l],
                                        conf_ln_g[l], conf_ln_b[l])
            q_dc = pc[I_DQ].reshape(B, L, DIFF_HEADS, 2, DIFF_D) * DIFF_D ** -0.5
            y_cc = diff_post(diff_core(q_dc, k_dc, v_dc, lam), diff_norm_g[l], lam_init)
            q_nc = pc[I_NQ].reshape(B, L, NA_HEADS, NA_D) * NA_D ** -0.5
            y_dc = dense_attn(q_nc, k_nc, v_nc)
            yc = merge_out(pc[I_GATE], (y_ac, y_bc, y_cc, y_dc), w_branch[l], b_branch[l], w_o[l])
            xc = layer_norm(DEEPNORM_ALPHA * xc + cg1 * yc, ln_g[l, 0], ln_b[l, 0])
            xc = layer_norm(DEEPNORM_ALPHA * xc + cg2 * sq_relu_ffn(xc * (1.0 + csc2) + csh2, w_ff1[l], w_ff2[l]),
                            ln_g[l, 1], ln_b[l, 1])
    return x
```

```python
import functools
import math

import jax
import jax.numpy as jnp
from jax import lax
from jax.experimental import pallas as pl
from jax.experimental.pallas import tpu as pltpu

F32 = jnp.float32
BF16 = jnp.bfloat16

D_MODEL = 1024
GRID_W = 64
BRANCH_W = D_MODEL // 2
N_BRANCH = 4
CONV_A_K = 3
CONF_K = 31
DIFF_HEADS = 4
DIFF_D = D_MODEL // 16
DIFF_V = 2 * DIFF_D
NA_HEADS = 8
NA_D = D_MODEL // 16
NA_WIN_R = 8
NA_WIN_C = 16
D_FF = 4 * D_MODEL
ROPE_BASE = 10000.0
LN_EPS = 1e-5
NEG_INF = -1e30

GATE_W = N_BRANCH * D_MODEL
MAIN_W = 11 * BRANCH_W
PROJ_W = GATE_W + MAIN_W
PROJ_TN = 512
T_GATE_END = GATE_W // PROJ_TN
T_AB, T_AC, T_AX, T_CA, T_CG, T_DQ, T_DK, T_DV, T_NQ, T_NK, T_NV = range(T_GATE_END, T_GATE_END + 11)
LANES = 128
C_DQ, C_DK, C_DV, C_NQ, C_NK, C_NV = (t * (PROJ_TN // LANES) for t in (T_DQ, T_DK, T_DV, T_NQ, T_NK, T_NV))

HALO = 16
NA_QROWS = 8
NA_KROWS = 16
VMEM_LIMIT = 56 * 1024 * 1024


def _params(*sem):
    return pltpu.CompilerParams(dimension_semantics=sem, vmem_limit_bytes=VMEM_LIMIT)


def _sigmoid(v):
    return 1.0 / (1.0 + jnp.exp(-v))


def _layer_norm(v, g, b):
    mu = jnp.mean(v, axis=-1, keepdims=True)
    d = v - mu
    var = jnp.mean(d * d, axis=-1, keepdims=True)
    return d * lax.rsqrt(var + LN_EPS) * g + b


def _ada_kernel(cond_ref, w_ref, b_ref, o_ref):
    cnd = cond_ref[...]
    act = (cnd * _sigmoid(cnd)).astype(BF16)
    o_ref[0] = jnp.dot(act, w_ref[0].astype(BF16), preferred_element_type=F32) + b_ref[0]


def _ada_call(cond, w_ada, b_ada):
    depth, d, n = w_ada.shape
    rows = cond.shape[0]
    tn = 1024
    return pl.pallas_call(
        _ada_kernel,
        grid=(depth, n // tn),
        in_specs=[
            pl.BlockSpec((rows, d), lambda l, j: (0, 0)),
            pl.BlockSpec((1, d, tn), lambda l, j: (l, 0, j)),
            pl.BlockSpec((1, 1, tn), lambda l, j: (l, 0, j)),
        ],
        out_specs=pl.BlockSpec((1, rows, tn), lambda l, j: (l, 0, j)),
        out_shape=jax.ShapeDtypeStruct((depth, rows, n), F32),
        compiler_params=_params("parallel", "parallel"),
        name="ada_mod",
    )(cond, w_ada, b_ada.reshape(depth, 1, n))


def _proj_kernel(*refs, rope):
    if rope:
        x_ref, mod_ref, w_ref, b_ref, cos_ref, sinm_ref, sinp_ref, o_ref, u_ref = refs
    else:
        x_ref, mod_ref, w_ref, b_ref, o_ref, u_ref = refs
    n = pl.program_id(1)

    @pl.when(n == 0)
    def _():
        sh = mod_ref[0, 0:1, :]
        sc = mod_ref[0, 1:2, :]
        u_ref[...] = (x_ref[...] * (1.0 + sc) + sh).astype(BF16)

    acc = jnp.dot(u_ref[...], w_ref[...], preferred_element_type=F32) + b_ref[...]
    is_q = jnp.logical_or(n == T_DQ, n == T_NQ)
    scale = jnp.where(is_q, DIFF_D ** -0.5, 1.0).astype(F32)
    if rope:
        is_rope = jnp.logical_or(n == T_DQ, n == T_DK)

        @pl.when(is_rope)
        def _():
            cs = cos_ref[...]
            sm = sinm_ref[...]
            sp = sinp_ref[...]
            for g in range(PROJ_TN // LANES):
                xg = acc[:, g * LANES:(g + 1) * LANES]
                rot = xg * cs + pltpu.roll(xg, LANES - DIFF_D // 4, 1) * sm + pltpu.roll(xg, DIFF_D // 4, 1) * sp
                o_ref[:, g * LANES:(g + 1) * LANES] = (rot * scale).astype(BF16)

        @pl.when(jnp.logical_not(is_rope))
        def _():
            o_ref[...] = (acc * scale).astype(BF16)
    else:
        o_ref[...] = (acc * scale).astype(BF16)


def _proj_call(x, mod, w, b, rope_tabs, seq, tm):
    m, d = x.shape
    ntile = PROJ_W // PROJ_TN
    tpm = seq // tm
    rope = rope_tabs is not None
    in_specs = [
        pl.BlockSpec((tm, d), lambda i, n: (i, 0)),
        pl.BlockSpec((1, 6, d), lambda i, n: (i // tpm, 0, 0)),
        pl.BlockSpec((d, PROJ_TN), lambda i, n: (0, n)),
        pl.BlockSpec((1, PROJ_TN), lambda i, n: (0, n)),
    ]
    args = [x, mod, w, b]
    if rope:
        in_specs += [pl.BlockSpec((tm, LANES), lambda i, n: (i % tpm, 0))] * 3
        args += list(rope_tabs)
    return pl.pallas_call(
        functools.partial(_proj_kernel, rope=rope),
        grid=(m // tm, ntile),
        in_specs=in_specs,
        out_specs=pl.BlockSpec((tm, PROJ_TN), lambda i, n: (i, n)),
        out_shape=jax.ShapeDtypeStruct((m, PROJ_W), BF16),
        scratch_shapes=[pltpu.VMEM((tm, d), BF16)],
        compiler_params=_params("parallel", "arbitrary"),
        name="proj_lat" if rope else "proj_ctx",
    )(*args)


def _local_kernel(ab, ac, ax, ca, cg, acp, axp, cap, cgp, acn, axn, can, cgn,
                  caw, cab, dww, dwb, lng, lnb, ya_ref, yb_ref, z_ref, glu_ref, conv_ref, *, tt, tps):
    i = pl.program_id(0)
    si = i % tps
    has_prev = (si > 0).astype(F32)
    has_next = (si < tps - 1).astype(F32)

    def glu(a, g):
        return a[...].astype(F32) * _sigmoid(g[...].astype(F32))

    def prod(a, b):
        return a[...].astype(F32) * b[...].astype(F32)

    glu_ref[0:HALO, :] = glu(cap, cgp) * has_prev
    glu_ref[HALO:HALO + tt, :] = glu(ca, cg)
    glu_ref[HALO + tt:, :] = glu(can, cgn) * has_next
    z_ref[0:HALO, :] = prod(acp, axp) * has_prev
    z_ref[HALO:HALO + tt, :] = prod(ac, ax)
    z_ref[HALO + tt:, :] = prod(acn, axn) * has_next

    rc = min(tt, 128)
    for g in range(BRANCH_W // LANES):
        ls = slice(g * LANES, (g + 1) * LANES)
        for r0 in range(0, tt, rc):
            acc = z_ref[pl.ds(r0 + HALO - 1, rc), ls] * caw[0:1, ls]
            for k in range(1, CONV_A_K):
                acc = acc + z_ref[pl.ds(r0 + HALO - 1 + k, rc), ls] * caw[k:k + 1, ls]
            ya = ab[pl.ds(r0, rc), ls].astype(F32) * (acc + cab[:, ls])
            ya_ref[pl.ds(r0, rc), ls] = ya.astype(BF16)

            off = HALO - CONF_K // 2
            acc = glu_ref[pl.ds(r0 + off, rc), ls] * dww[0:1, ls]
            for k in range(1, CONF_K):
                acc = acc + glu_ref[pl.ds(r0 + off + k, rc), ls] * dww[k:k + 1, ls]
            conv_ref[pl.ds(r0, rc), ls] = acc + dwb[:, ls]

    yn = _layer_norm(conv_ref[...], lng[...], lnb[...])
    yb_ref[...] = (yn * _sigmoid(yn)).astype(BF16)


def _local_call(p, caw, cab, dww, dwb, lng, lnb, seq, tt):
    m = p.shape[0]
    tps = seq // tt
    hb = tt // HALO
    last = m // HALO - 1

    def cur(t):
        return pl.BlockSpec((tt, BRANCH_W), lambda i: (i, t))

    def prev(t):
        return pl.BlockSpec((HALO, BRANCH_W), lambda i: (jnp.maximum(i * hb - 1, 0), t))

    def nxt(t):
        return pl.BlockSpec((HALO, BRANCH_W), lambda i: (jnp.minimum((i + 1) * hb, last), t))

    def full(a):
        return pl.BlockSpec(a.shape, lambda i: (0,) * a.ndim)

    small = [caw, cab.reshape(1, -1), dww, dwb.reshape(1, -1), lng.reshape(1, -1), lnb.reshape(1, -1)]
    in_specs = ([cur(t) for t in (T_AB, T_AC, T_AX, T_CA, T_CG)]
                + [prev(t) for t in (T_AC, T_AX, T_CA, T_CG)]
                + [nxt(t) for t in (T_AC, T_AX, T_CA, T_CG)]
                + [full(a) for a in small])
    out = jax.ShapeDtypeStruct((m, BRANCH_W), BF16)
    return pl.pallas_call(
        functools.partial(_local_kernel, tt=tt, tps=tps),
        grid=(m // tt,),
        in_specs=in_specs,
        out_specs=[pl.BlockSpec((tt, BRANCH_W), lambda i: (i, 0))] * 2,
        out_shape=[out, out],
        scratch_shapes=[pltpu.VMEM((tt + 2 * HALO, BRANCH_W), F32),
                        pltpu.VMEM((tt + 2 * HALO, BRANCH_W), F32),
                        pltpu.VMEM((tt, BRANCH_W), F32)],
        compiler_params=_params("parallel"),
        name="local_branches",
    )(*([p] * 13), *small)


def _diff_kernel(*refs, tq, tk, n_lat, lam_init):
    if n_lat:
        dl_ref, q_ref, kc_ref, vc_ref, k_ref, v_ref, g_ref, o_ref, m_ref, l_ref, acc_ref = refs
    else:
        dl_ref, q_ref, kc_ref, vc_ref, g_ref, o_ref, m_ref, l_ref, acc_ref = refs
    dl = dl_ref[...]
    lam = (jnp.exp(jnp.sum(dl[0:1] * dl[1:2], axis=-1, keepdims=True))
           - jnp.exp(jnp.sum(dl[2:3] * dl[3:4], axis=-1, keepdims=True)) + lam_init)

    qf = q_ref[...].astype(F32)
    lane = lax.broadcasted_iota(jnp.int32, qf.shape, 1)
    q2 = jnp.concatenate([jnp.where(lane < DIFF_D, qf, 0.0), jnp.where(lane >= DIFF_D, qf, 0.0)],
                         axis=0).astype(BF16)

    m_ref[...] = jnp.full(m_ref.shape, NEG_INF, F32)
    l_ref[...] = jnp.zeros(l_ref.shape, F32)
    acc_ref[...] = jnp.zeros(acc_ref.shape, F32)

    def step(kb, vb):
        s = lax.dot_general(q2, kb, (((1,), (1,)), ((), ())), preferred_element_type=F32)
        m_old = m_ref[...]
        m_new = jnp.maximum(m_old, jnp.max(s, axis=-1, keepdims=True))
        alpha = jnp.exp(m_old - m_new)
        p = jnp.exp(s - m_new)
        l_ref[...] = alpha * l_ref[...] + jnp.sum(p, axis=-1, keepdims=True)
        acc_ref[...] = alpha * acc_ref[...] + jnp.dot(p.astype(BF16), vb, preferred_element_type=F32)
        m_ref[...] = m_new

    step(kc_ref[...], vc_ref[...])
    if n_lat:
        def body(j, carry):
            j0 = pl.multiple_of(j * tk, tk)
            step(k_ref[pl.ds(j0, tk), :], v_ref[pl.ds(j0, tk), :])
            return carry
        lax.fori_loop(0, n_lat // tk, body, 0)

    o = acc_ref[...] / l_ref[...]
    od = o[:tq] - lam * o[tq:]
    ms = jnp.mean(od * od, axis=-1, keepdims=True)
    o_ref[...] = (od * lax.rsqrt(ms + LN_EPS) * g_ref[...] * (1.0 - lam_init)).astype(BF16)


def _diff_call(dl, pq, pc, pk, g, lam_init, batch, q_seq, ctx_len, lat_seq, tq, tk):
    nq = q_seq // tq
    in_specs = [
        pl.BlockSpec(dl.shape, lambda b, h, i: (0, 0)),
        pl.BlockSpec((tq, LANES), lambda b, h, i: (b * nq + i, C_DQ + h)),
        pl.BlockSpec((ctx_len, LANES), lambda b, h, i: (b, C_DK + h)),
        pl.BlockSpec((ctx_len, LANES), lambda b, h, i: (b, C_DV + h)),
    ]
    args = [dl, pq, pc, pc]
    if pk is not None:
        in_specs += [pl.BlockSpec((lat_seq, LANES), lambda b, h, i: (b, C_DK + h)),
                     pl.BlockSpec((lat_seq, LANES), lambda b, h, i: (b, C_DV + h))]
        args += [pk, pk]
    in_specs.append(pl.BlockSpec((1, DIFF_V), lambda b, h, i: (0, 0)))
    args.append(g.reshape(1, DIFF_V))
    return pl.pallas_call(
        functools.partial(_diff_kernel, tq=tq, tk=tk, n_lat=lat_seq if pk is not None else 0, lam_init=lam_init),
        grid=(batch, DIFF_HEADS, nq),
        in_specs=in_specs,
        out_specs=pl.BlockSpec((tq, DIFF_V), lambda b, h, i: (b * nq + i, h)),
        out_shape=jax.ShapeDtypeStruct((batch * q_seq, BRANCH_W), BF16),
        scratch_shapes=[pltpu.VMEM((2 * tq, 1), F32), pltpu.VMEM((2 * tq, 1), F32),
                        pltpu.VMEM((2 * tq, DIFF_V), F32)],
        compiler_params=_params("parallel", "parallel", "arbitrary"),
        name="diff_attn_lat" if pk is not None else "diff_attn_ctx",
    )(*args)


def _na_kernel(*refs, rows, lat):
    if lat:
        q_ref, kc_ref, vc_ref, k_ref, v_ref, bias_ref, o_ref = refs
    else:
        q_ref, kc_ref, vc_ref, o_ref = refs
    qf = q_ref[...].astype(F32)
    lane = lax.broadcasted_iota(jnp.int32, qf.shape, 1)
    kc = kc_ref[...]
    vc = vc_ref[...]
    if lat:
        i = pl.program_id(2)
        kw0 = jnp.clip(i * NA_QROWS - NA_WIN_R // 2, 0, rows - NA_KROWS)
        start = pl.multiple_of(kw0 * GRID_W, NA_WIN_R // 2 * GRID_W)
        kw = k_ref[pl.ds(start, NA_KROWS * GRID_W), :]
        vw = v_ref[pl.ds(start, NA_KROWS * GRID_W), :]
    nt = (((1,), (1,)), ((), ()))
    outs = []
    for hh in range(2):
        sel = (lane < NA_D) if hh == 0 else (lane >= NA_D)
        qz = jnp.where(sel, qf, 0.0).astype(BF16)
        s_c = lax.dot_general(qz, kc, nt, preferred_element_type=F32)
        m = jnp.max(s_c, axis=-1, keepdims=True)
        if lat:
            s_l = lax.dot_general(qz, kw, nt, preferred_element_type=F32) + bias_ref[0, hh]
            m = jnp.maximum(m, jnp.max(s_l, axis=-1, keepdims=True))
        p_c = jnp.exp(s_c - m)
        l = jnp.sum(p_c, axis=-1, keepdims=True)
        o = jnp.dot(p_c.astype(BF16), vc, preferred_element_type=F32)
        if lat:
            p_l = jnp.exp(s_l - m)
            l = l + jnp.sum(p_l, axis=-1, keepdims=True)
            o = o + jnp.dot(p_l.astype(BF16), vw, preferred_element_type=F32)
        outs.append(o / l)
    o_ref[...] = jnp.where(lane < NA_D, outs[0], outs[1]).astype(BF16)


def _na_bias_table(rpb, rows):
    win_r = NA_WIN_R
    cidx = jnp.arange(GRID_W)
    cstart = jnp.clip(cidx - NA_WIN_C // 2, 0, GRID_W - NA_WIN_C)
    col_ok = (cidx[None, :] >= cstart[:, None]) & (cidx[None, :] < cstart[:, None] + NA_WIN_C)
    dc_idx = jnp.clip(cidx[None, :] - cidx[:, None] + NA_WIN_C - 1, 0, 2 * NA_WIN_C - 2)
    tabs = []
    for r_b in (0, NA_QROWS, rows - NA_QROWS):
        kw0 = min(max(r_b - win_r // 2, 0), rows - NA_KROWS)
        qr = r_b + jnp.arange(NA_QROWS)
        r0 = jnp.clip(qr - win_r // 2, 0, rows - win_r)
        kr = kw0 + jnp.arange(NA_KROWS)
        row_ok = (kr[None, :] >= r0[:, None]) & (kr[None, :] < r0[:, None] + win_r)
        dr_idx = jnp.clip(kr[None, :] - qr[:, None] + win_r - 1, 0, 2 * win_r - 2)
        t = rpb[:, dr_idx[:, None, :, None], dc_idx[None, :, None, :]]
        ok = row_ok[:, None, :, None] & col_ok[None, :, None, :]
        t = jnp.where(ok[None], t.astype(F32), NEG_INF)
        tabs.append(t.reshape(rpb.shape[0], NA_QROWS * GRID_W, NA_KROWS * GRID_W))
    return jnp.stack(tabs)


def _na_lat_call(p, pc, bias, batch, seq, ctx_len):
    rows = seq // GRID_W
    tq = NA_QROWS * GRID_W
    nblk = seq // tq

    def pat(i):
        return jnp.where(i == 0, 0, jnp.where(i == nblk - 1, 2, 1))

    return pl.pallas_call(
        functools.partial(_na_kernel, rows=rows, lat=True),
        grid=(batch, NA_HEADS // 2, nblk),
        in_specs=[
            pl.BlockSpec((tq, LANES), lambda b, h, i: (b * nblk + i, C_NQ + h)),
            pl.BlockSpec((ctx_len, LANES), lambda b, h, i: (b, C_NK + h)),
            pl.BlockSpec((ctx_len, LANES), lambda b, h, i: (b, C_NV + h)),
            pl.BlockSpec((seq, LANES), lambda b, h, i: (b, C_NK + h)),
            pl.BlockSpec((seq, LANES), lambda b, h, i: (b, C_NV + h)),
            pl.BlockSpec((1, 2, tq, NA_KROWS * GRID_W), lambda b, h, i: (pat(i), h, 0, 0)),
        ],
        out_specs=pl.BlockSpec((tq, LANES), lambda b, h, i: (b * nblk + i, h)),
        out_shape=jax.ShapeDtypeStruct((batch * seq, BRANCH_W), BF16),
        compiler_params=_params("parallel", "parallel", "arbitrary"),
        name="na_attn_lat",
    )(p, pc, pc, p, p, bias)


def _na_ctx_call(pc, batch, ctx_len):
    return pl.pallas_call(
        functools.partial(_na_kernel, rows=0, lat=False),
        grid=(batch, NA_HEADS // 2),
        in_specs=[
            pl.BlockSpec((ctx_len, LANES), lambda b, h: (b, C_NQ + h)),
            pl.BlockSpec((ctx_len, LANES), lambda b, h: (b, C_NK + h)),
            pl.BlockSpec((ctx_len, LANES), lambda b, h: (b, C_NV + h)),
        ],
        out_specs=pl.BlockSpec((ctx_len, LANES), lambda b, h: (b, h)),
        out_shape=jax.ShapeDtypeStruct((batch * ctx_len, BRANCH_W), BF16),
        compiler_params=_params("parallel", "parallel"),
        name="na_attn_ctx",
    )(pc, pc, pc)


def _merge_kernel(x_ref, mod_ref, g0, g1, g2, g3, y0, y1, y2, y3, wb_ref, bb_ref, wo_ref, lng, lnb, o_ref, *, alpha):
    merged = None
    for idx, (g_ref, y_ref) in enumerate(((g0, y0), (g1, y1), (g2, y2), (g3, y3))):
        br = jnp.dot(y_ref[...], wb_ref[idx], preferred_element_type=F32) + bb_ref[idx:idx + 1, :]
        term = _sigmoid(g_ref[...].astype(F32)) * br
        merged = term if merged is None else merged + term
    y = jnp.dot(merged.astype(BF16), wo_ref[...], preferred_element_type=F32)
    gate = mod_ref[0, 2:3, :]
    o_ref[...] = _layer_norm(alpha * x_ref[...] + gate * y, lng[...], lnb[...])


def _merge_call(x, mod, p, ys, wb, bb, wo, lng, lnb, seq, tm, alpha):
    m, d = x.shape
    tpm = seq // tm
    in_specs = ([pl.BlockSpec((tm, d), lambda i: (i, 0)),
                 pl.BlockSpec((1, 6, d), lambda i: (i // tpm, 0, 0))]
                + [pl.BlockSpec((tm, d), functools.partial(lambda i, k: (i, k), k=k)) for k in range(N_BRANCH)]
                + [pl.BlockSpec((tm, BRANCH_W), lambda i: (i, 0))] * N_BRANCH
                + [pl.BlockSpec(wb.shape, lambda i: (0, 0, 0)),
                   pl.BlockSpec(bb.shape, lambda i: (0, 0)),
                   pl.BlockSpec(wo.shape, lambda i: (0, 0)),
                   pl.BlockSpec((1, d), lambda i: (0, 0)),
                   pl.BlockSpec((1, d), lambda i: (0, 0))])
    return pl.pallas_call(
        functools.partial(_merge_kernel, alpha=alpha),
        grid=(m // tm,),
        in_specs=in_specs,
        out_specs=pl.BlockSpec((tm, d), lambda i: (i, 0)),
        out_shape=jax.ShapeDtypeStruct((m, d), F32),
        compiler_params=_params("parallel"),
        name="merge_out",
    )(x, mod, p, p, p, p, *ys, wb, bb, wo, lng.reshape(1, d), lnb.reshape(1, d))


def _ffn_kernel(x_ref, mod_ref, w1_ref, w2_ref, lng, lnb, o_ref, h_ref, acc_ref, *, alpha):
    k = pl.program_id(1)

    @pl.when(k == 0)
    def _():
        sh = mod_ref[0, 3:4, :]
        sc = mod_ref[0, 4:5, :]
        h_ref[...] = (x_ref[...] * (1.0 + sc) + sh).astype(BF16)
        acc_ref[...] = jnp.zeros(acc_ref.shape, F32)

    a = jnp.maximum(jnp.dot(h_ref[...], w1_ref[...], preferred_element_type=F32), 0.0)
    acc_ref[...] += jnp.dot((a * a).astype(BF16), w2_ref[...], preferred_element_type=F32)

    @pl.when(k == pl.num_programs(1) - 1)
    def _():
        gate = mod_ref[0, 5:6, :]
        o_ref[...] = _layer_norm(alpha * x_ref[...] + gate * acc_ref[...], lng[...], lnb[...])


def _ffn_call(x, mod, w1, w2, lng, lnb, seq, tm, kf, alpha):
    m, d = x.shape
    tpm = seq // tm
    return pl.pallas_call(
        functools.partial(_ffn_kernel, alpha=alpha),
        grid=(m // tm, D_FF // kf),
        in_specs=[
            pl.BlockSpec((tm, d), lambda i, k: (i, 0)),
            pl.BlockSpec((1, 6, d), lambda i, k: (i // tpm, 0, 0)),
            pl.BlockSpec((d, kf), lambda i, k: (0, k)),
            pl.BlockSpec((kf, d), lambda i, k: (k, 0)),
            pl.BlockSpec((1, d), lambda i, k: (0, 0)),
            pl.BlockSpec((1, d), lambda i, k: (0, 0)),
        ],
        out_specs=pl.BlockSpec((tm, d), lambda i, k: (i, 0)),
        out_shape=jax.ShapeDtypeStruct((m, d), F32),
        scratch_shapes=[pltpu.VMEM((tm, d), BF16), pltpu.VMEM((tm, d), F32)],
        compiler_params=_params("parallel", "arbitrary"),
        name="ffn",
    )(x, mod, w1, w2, lng.reshape(1, d), lnb.reshape(1, d))


def _rope_tables(seq):
    nf = DIFF_D // 4
    t = jnp.arange(seq)
    freqs = jnp.power(ROPE_BASE, -jnp.arange(nf, dtype=F32) / nf)
    pos = jnp.stack([t // GRID_W, t % GRID_W], axis=-1).astype(F32)
    ang = pos[:, :, None] * freqs
    cos, sin = jnp.cos(ang), jnp.sin(ang)
    shape = (seq, LANES // DIFF_D, 2, 2, nf)
    cos_t = jnp.broadcast_to(cos[:, None, :, None, :], shape)
    sin_t = jnp.broadcast_to(sin[:, None, :, None, :], shape)
    half = jnp.arange(2)[None, None, None, :, None]
    sin_m = jnp.where(half == 0, -sin_t, 0.0)
    sin_p = jnp.where(half == 1, sin_t, 0.0)
    return tuple(a.reshape(seq, LANES) for a in (cos_t, sin_m, sin_p))


def kernel(x, c, ctx, c_ctx, w_ada, b_ada, w_in, b_in, conv_a_w, conv_a_b, conf_dw_w, conf_dw_b, conf_ln_g, conf_ln_b, diff_lambda, diff_norm_g, na_rpb, w_branch, b_branch, w_o, ln_g, ln_b, w_ff1, w_ff2):
    batch, seq, d = x.shape
    ctx_len = ctx.shape[1]
    depth = w_ada.shape[0]
    rows = seq // GRID_W
    assert d == D_MODEL and seq % (NA_QROWS * GRID_W) == 0 and rows >= NA_KROWS
    alpha = (2 * depth) ** 0.25

    cond_rows = -(-(batch + 1) // 8) * 8
    cond = jnp.zeros((cond_rows, d), F32).at[:batch].set(c).at[batch].set(c_ctx)
    mods = _ada_call(cond, w_ada, b_ada)

    w_in_p = jnp.concatenate([w_in[:, :, MAIN_W:], w_in[:, :, :MAIN_W]], axis=-1).astype(BF16)
    b_in_p = jnp.concatenate([b_in[:, MAIN_W:], b_in[:, :MAIN_W]], axis=-1).reshape(depth, 1, PROJ_W)
    w_branch_b = w_branch.astype(BF16)
    w_o_b = w_o.astype(BF16)
    w_ff1_b = w_ff1.astype(BF16)
    w_ff2_b = w_ff2.astype(BF16)
    rope_tabs = _rope_tables(seq)

    tm_lat = min(1024, seq)
    tq = min(256, seq)
    tk = min(512, seq)
    m_ctx = batch * ctx_len

    xl = x.reshape(batch * seq, d)
    xc = ctx.reshape(m_ctx, d)
    for l in range(depth):
        last = l == depth - 1
        lam_init = 0.8 - 0.6 * math.exp(-0.3 * l)
        mod_lat = mods[l, :batch].reshape(batch, 6, d)
        mod_ctx = mods[l, batch:batch + 1].reshape(1, 6, d)
        local_w = (conv_a_w[l], conv_a_b[l], conf_dw_w[l], conf_dw_b[l], conf_ln_g[l], conf_ln_b[l])

        pc = _proj_call(xc, mod_ctx, w_in_p[l], b_in_p[l], None, m_ctx, m_ctx)
        p = _proj_call(xl, mod_lat, w_in_p[l], b_in_p[l], rope_tabs, seq, tm_lat)

        y_a, y_b = _local_call(p, *local_w, seq, min(512, seq))
        y_c = _diff_call(diff_lambda[l], p, pc, p, diff_norm_g[l], lam_init, batch, seq, ctx_len, seq, tq, tk)
        y_d = _na_lat_call(p, pc, _na_bias_table(na_rpb[l], rows), batch, seq, ctx_len)
        x1 = _merge_call(xl, mod_lat, p, (y_a, y_b, y_c, y_d), w_branch_b[l], b_branch[l], w_o_b[l],
                         ln_g[l, 0], ln_b[l, 0], seq, min(512, seq), alpha)
        xl = _ffn_call(x1, mod_lat, w_ff1_b[l], w_ff2_b[l], ln_g[l, 1], ln_b[l, 1], seq, tm_lat, 512, alpha)

        if not last:
            yc_a, yc_b = _local_call(pc, *local_w, ctx_len, ctx_len)
            yc_c = _diff_call(diff_lambda[l], pc, pc, None, diff_norm_g[l], lam_init, batch, ctx_len, ctx_len, 0,
                              ctx_len, tk)
            yc_d = _na_ctx_call(pc, batch, ctx_len)
            xc1 = _merge_call(xc, mod_ctx, pc, (yc_a, yc_b, yc_c, yc_d), w_branch_b[l], b_branch[l], w_o_b[l],
                              ln_g[l, 0], ln_b[l, 0], m_ctx, min(512, m_ctx), alpha)
            xc = _ffn_call(xc1, mod_ctx, w_ff1_b[l], w_ff2_b[l], ln_g[l, 1], ln_b[l, 1], m_ctx, min(1024, m_ctx),
                           512, alpha)
    return xl.reshape(batch, seq, d)
```

```python
import functools
import math

import jax
import jax.numpy as jnp
from jax import lax
from jax.experimental import pallas as pl
from jax.experimental.pallas import tpu as pltpu

F32 = jnp.float32
BF16 = jnp.bfloat16

D_MODEL = 1024
GRID_W = 64
BRANCH_W = D_MODEL // 2
N_BRANCH = 4
CONV_A_K = 3
CONF_K = 31
DIFF_HEADS = 4
DIFF_D = D_MODEL // 16
DIFF_V = 2 * DIFF_D
NA_HEADS = 8
NA_D = D_MODEL // 16
NA_WIN_R = 8
NA_WIN_C = 16
D_FF = 4 * D_MODEL
ROPE_BASE = 10000.0
LN_EPS = 1e-5
NEG_INF = -1e30

GATE_W = N_BRANCH * D_MODEL
MAIN_W = 11 * BRANCH_W
PROJ_W = GATE_W + MAIN_W
PROJ_TN = 512
T_GATE_END = GATE_W // PROJ_TN
T_AB, T_AC, T_AX, T_CA, T_CG, T_DQ, T_DK, T_DV, T_NQ, T_NK, T_NV = range(T_GATE_END, T_GATE_END + 11)
LANES = 128
C_DQ, C_DK, C_DV, C_NQ, C_NK, C_NV = (t * (PROJ_TN // LANES) for t in (T_DQ, T_DK, T_DV, T_NQ, T_NK, T_NV))

HALO = 16
NA_QROWS = 8
NA_KROWS = 16
VMEM_LIMIT = 56 * 1024 * 1024


def _params(*sem):
    return pltpu.CompilerParams(dimension_semantics=sem, vmem_limit_bytes=VMEM_LIMIT)


def _sigmoid(v):
    return 1.0 / (1.0 + jnp.exp(-v))


def _layer_norm(v, g, b):
    mu = jnp.mean(v, axis=-1, keepdims=True)
    d = v - mu
    var = jnp.mean(d * d, axis=-1, keepdims=True)
    return d * lax.rsqrt(var + LN_EPS) * g + b


def _ada_kernel(cond_ref, w_ref, b_ref, o_ref):
    cnd = cond_ref[...]
    act = (cnd * _sigmoid(cnd)).astype(BF16)
    o_ref[0] = jnp.dot(act, w_ref[0].astype(BF16), preferred_element_type=F32) + b_ref[0]


def _ada_call(cond, w_ada, b_ada):
    depth, d, n = w_ada.shape
    rows = cond.shape[0]
    tn = 1024
    return pl.pallas_call(
        _ada_kernel,
        grid=(depth, n // tn),
        in_specs=[
            pl.BlockSpec((rows, d), lambda l, j: (0, 0)),
            pl.BlockSpec((1, d, tn), lambda l, j: (l, 0, j)),
            pl.BlockSpec((1, 1, tn), lambda l, j: (l, 0, j)),
        ],
        out_specs=pl.BlockSpec((1, rows, tn), lambda l, j: (l, 0, j)),
        out_shape=jax.ShapeDtypeStruct((depth, rows, n), F32),
        compiler_params=_params("parallel", "parallel"),
        name="ada_mod",
    )(cond, w_ada, b_ada.reshape(depth, 1, n))


def _proj_kernel(*refs, rope):
    if rope:
        x_ref, mod_ref, w_ref, b_ref, cos_ref, sinm_ref, sinp_ref, o_ref, u_ref = refs
    else:
        x_ref, mod_ref, w_ref, b_ref, o_ref, u_ref = refs
    n = pl.program_id(1)

    @pl.when(n == 0)
    def _():
        sh = mod_ref[0, 0:1, :]
        sc = mod_ref[0, 1:2, :]
        u_ref[...] = (x_ref[...] * (1.0 + sc) + sh).astype(BF16)

    acc = jnp.dot(u_ref[...], w_ref[...], preferred_element_type=F32) + b_ref[...]
    is_q = jnp.logical_or(n == T_DQ, n == T_NQ)
    scale = jnp.where(is_q, DIFF_D ** -0.5, 1.0).astype(F32)
    if rope:
        is_rope = jnp.logical_or(n == T_DQ, n == T_DK)

        @pl.when(is_rope)
        def _():
            cs = cos_ref[...]
            sm = sinm_ref[...]
            sp = sinp_ref[...]
            for g in range(PROJ_TN // LANES):
                xg = acc[:, g * LANES:(g + 1) * LANES]
                rot = xg * cs + pltpu.roll(xg, LANES - DIFF_D // 4, 1) * sm + pltpu.roll(xg, DIFF_D // 4, 1) * sp
                o_ref[:, g * LANES:(g + 1) * LANES] = (rot * scale).astype(BF16)

        @pl.when(jnp.logical_not(is_rope))
        def _():
            o_ref[...] = (acc * scale).astype(BF16)
    else:
        o_ref[...] = (acc * scale).astype(BF16)


def _proj_call(x, mod, w, b, rope_tabs, seq, tm):
    m, d = x.shape
    ntile = PROJ_W // PROJ_TN
    tpm = seq // tm
    rope = rope_tabs is not None
    in_specs = [
        pl.BlockSpec((tm, d), lambda i, n: (i, 0)),
        pl.BlockSpec((1, 6, d), lambda i, n: (i // tpm, 0, 0)),
        pl.BlockSpec((d, PROJ_TN), lambda i, n: (0, n)),
        pl.BlockSpec((1, PROJ_TN), lambda i, n: (0, n)),
    ]
    args = [x, mod, w, b]
    if rope:
        in_specs += [pl.BlockSpec((tm, LANES), lambda i, n: (i % tpm, 0))] * 3
        args += list(rope_tabs)
    return pl.pallas_call(
        functools.partial(_proj_kernel, rope=rope),
        grid=(m // tm, ntile),
        in_specs=in_specs,
        out_specs=pl.BlockSpec((tm, PROJ_TN), lambda i, n: (i, n)),
        out_shape=jax.ShapeDtypeStruct((m, PROJ_W), BF16),
        scratch_shapes=[pltpu.VMEM((tm, d), BF16)],
        compiler_params=_params("parallel", "arbitrary"),
        name="proj_lat" if rope else "proj_ctx",
    )(*args)


def _local_kernel(ab, ac, ax, ca, cg, acp, axp, cap, cgp, acn, axn, can, cgn,
                  caw, cab, dww, dwb, lng, lnb, ya_ref, yb_ref, z_ref, glu_ref, conv_ref, *, tt, tps):
    i = pl.program_id(0)
    si = i % tps
    has_prev = (si > 0).astype(F32)
    has_next = (si < tps - 1).astype(F32)

    def glu(a, g):
        return a[...].astype(F32) * _sigmoid(g[...].astype(F32))

    def prod(a, b):
        return a[...].astype(F32) * b[...].astype(F32)

    glu_ref[0:HALO, :] = glu(cap, cgp) * has_prev
    glu_ref[HALO:HALO + tt, :] = glu(ca, cg)
    glu_ref[HALO + tt:, :] = glu(can, cgn) * has_next
    z_ref[0:HALO, :] = prod(acp, axp) * has_prev
    z_ref[HALO:HALO + tt, :] = prod(ac, ax)
    z_ref[HALO + tt:, :] = prod(acn, axn) * has_next

    rc = min(tt, 128)
    for g in range(BRANCH_W // LANES):
        ls = slice(g * LANES, (g + 1) * LANES)
        for r0 in range(0, tt, rc):
            acc = z_ref[pl.ds(r0 + HALO - 1, rc), ls] * caw[0:1, ls]
            for k in range(1, CONV_A_K):
                acc = acc + z_ref[pl.ds(r0 + HALO - 1 + k, rc), ls] * caw[k:k + 1, ls]
            ya = ab[pl.ds(r0, rc), ls].astype(F32) * (acc + cab[:, ls])
            ya_ref[pl.ds(r0, rc), ls] = ya.astype(BF16)

            off = HALO - CONF_K // 2
            acc = glu_ref[pl.ds(r0 + off, rc), ls] * dww[0:1, ls]
            for k in range(1, CONF_K):
                acc = acc + glu_ref[pl.ds(r0 + off + k, rc), ls] * dww[k:k + 1, ls]
            conv_ref[pl.ds(r0, rc), ls] = acc + dwb[:, ls]

    yn = _layer_norm(conv_ref[...], lng[...], lnb[...])
    yb_ref[...] = (yn * _sigmoid(yn)).astype(BF16)


def _local_call(p, caw, cab, dww, dwb, lng, lnb, seq, tt):
    m = p.shape[0]
    tps = seq // tt
    hb = tt // HALO
    last = m // HALO - 1

    def cur(t):
        return pl.BlockSpec((tt, BRANCH_W), lambda i: (i, t))

    def prev(t):
        return pl.BlockSpec((HALO, BRANCH_W), lambda i: (jnp.maximum(i * hb - 1, 0), t))

    def nxt(t):
        return pl.BlockSpec((HALO, BRANCH_W), lambda i: (jnp.minimum((i + 1) * hb, last), t))

    def full(a):
        return pl.BlockSpec(a.shape, lambda i: (0,) * a.ndim)

    small = [caw, cab.reshape(1, -1), dww, dwb.reshape(1, -1), lng.reshape(1, -1), lnb.reshape(1, -1)]
    in_specs = ([cur(t) for t in (T_AB, T_AC, T_AX, T_CA, T_CG)]
                + [prev(t) for t in (T_AC, T_AX, T_CA, T_CG)]
                + [nxt(t) for t in (T_AC, T_AX, T_CA, T_CG)]
                + [full(a) for a in small])
    out = jax.ShapeDtypeStruct((m, BRANCH_W), BF16)
    return pl.pallas_call(
        functools.partial(_local_kernel, tt=tt, tps=tps),
        grid=(m // tt,),
        in_specs=in_specs,
        out_specs=[pl.BlockSpec((tt, BRANCH_W), lambda i: (i, 0))] * 2,
        out_shape=[out, out],
        scratch_shapes=[pltpu.VMEM((tt + 2 * HALO, BRANCH_W), F32),
                        pltpu.VMEM((tt + 2 * HALO, BRANCH_W), F32),
                        pltpu.VMEM((tt, BRANCH_W), F32)],
        compiler_params=_params("parallel"),
        name="local_branches",
    )(*([p] * 13), *small)


def _diff_kernel(*refs, tq, tk, n_lat, lam_init):
    if n_lat:
        dl_ref, q_ref, kc_ref, vc_ref, k_ref, v_ref, g_ref, o_ref, m_ref, l_ref, acc_ref = refs
    else:
        dl_ref, q_ref, kc_ref, vc_ref, g_ref, o_ref, m_ref, l_ref, acc_ref = refs
    dl = dl_ref[...]
    lam = (jnp.exp(jnp.sum(dl[0:1] * dl[1:2], axis=-1, keepdims=True))
           - jnp.exp(jnp.sum(dl[2:3] * dl[3:4], axis=-1, keepdims=True)) + lam_init)

    qf = q_ref[...].astype(F32)
    lane = lax.broadcasted_iota(jnp.int32, qf.shape, 1)
    q2 = jnp.concatenate([jnp.where(lane < DIFF_D, qf, 0.0), jnp.where(lane >= DIFF_D, qf, 0.0)],
                         axis=0).astype(BF16)

    m_ref[...] = jnp.full(m_ref.shape, NEG_INF, F32)
    l_ref[...] = jnp.zeros(l_ref.shape, F32)
    acc_ref[...] = jnp.zeros(acc_ref.shape, F32)

    def step(kb, vb):
        nc = kb.shape[0] // LANES
        s = lax.dot_general(q2, kb, (((1,), (1,)), ((), ())), preferred_element_type=F32)
        cols = [s[:, c * LANES:(c + 1) * LANES] for c in range(nc)]
        cmax = cols[0]
        for sc in cols[1:]:
            cmax = jnp.maximum(cmax, sc)
        m_old = m_ref[...]
        m_new = jnp.maximum(m_old, jnp.max(cmax, axis=1, keepdims=True))
        alpha = jnp.exp(m_old - m_new)
        ps = [jnp.exp(sc - m_new) for sc in cols]
        psum = ps[0]
        for pc in ps[1:]:
            psum = psum + pc
        p = jnp.concatenate(ps, axis=1).astype(BF16)
        l_ref[...] = alpha * l_ref[...] + psum
        acc_ref[...] = alpha * acc_ref[...] + jnp.dot(p, vb, preferred_element_type=F32)
        m_ref[...] = m_new

    step(kc_ref[...], vc_ref[...])
    if n_lat:
        def body(j, carry):
            j0 = pl.multiple_of(j * tk, tk)
            step(k_ref[pl.ds(j0, tk), :], v_ref[pl.ds(j0, tk), :])
            return carry
        lax.fori_loop(0, n_lat // tk, body, 0)

    o = acc_ref[...] / jnp.sum(l_ref[...], axis=1, keepdims=True)
    od = o[:tq] - lam * o[tq:]
    ms = jnp.mean(od * od, axis=-1, keepdims=True)
    o_ref[...] = (od * lax.rsqrt(ms + LN_EPS) * g_ref[...] * (1.0 - lam_init)).astype(BF16)


def _diff_call(dl, pq, pc, pk, g, lam_init, batch, q_seq, ctx_len, lat_seq, tq, tk):
    nq = q_seq // tq
    in_specs = [
        pl.BlockSpec(dl.shape, lambda b, h, i: (0, 0)),
        pl.BlockSpec((tq, LANES), lambda b, h, i: (b * nq + i, C_DQ + h)),
        pl.BlockSpec((ctx_len, LANES), lambda b, h, i: (b, C_DK + h)),
        pl.BlockSpec((ctx_len, LANES), lambda b, h, i: (b, C_DV + h)),
    ]
    args = [dl, pq, pc, pc]
    if pk is not None:
        in_specs += [pl.BlockSpec((lat_seq, LANES), lambda b, h, i: (b, C_DK + h)),
                     pl.BlockSpec((lat_seq, LANES), lambda b, h, i: (b, C_DV + h))]
        args += [pk, pk]
    in_specs.append(pl.BlockSpec((1, DIFF_V), lambda b, h, i: (0, 0)))
    args.append(g.reshape(1, DIFF_V))
    return pl.pallas_call(
        functools.partial(_diff_kernel, tq=tq, tk=tk, n_lat=lat_seq if pk is not None else 0, lam_init=lam_init),
        grid=(batch, DIFF_HEADS, nq),
        in_specs=in_specs,
        out_specs=pl.BlockSpec((tq, DIFF_V), lambda b, h, i: (b * nq + i, h)),
        out_shape=jax.ShapeDtypeStruct((batch * q_seq, BRANCH_W), BF16),
        scratch_shapes=[pltpu.VMEM((2 * tq, LANES), F32), pltpu.VMEM((2 * tq, LANES), F32),
                        pltpu.VMEM((2 * tq, DIFF_V), F32)],
        compiler_params=_params("parallel", "parallel", "arbitrary"),
        name="diff_attn_lat" if pk is not None else "diff_attn_ctx",
    )(*args)


def _na_kernel(*refs, rows, lat):
    if lat:
        q_ref, kc_ref, vc_ref, k_ref, v_ref, bias_ref, o_ref = refs
    else:
        q_ref, kc_ref, vc_ref, o_ref = refs
    qf = q_ref[...].astype(F32)
    lane = lax.broadcasted_iota(jnp.int32, qf.shape, 1)
    kc = kc_ref[...]
    vc = vc_ref[...]
    if lat:
        i = pl.program_id(2)
        kw0 = jnp.clip(i * NA_QROWS - NA_WIN_R // 2, 0, rows - NA_KROWS)
        start = pl.multiple_of(kw0 * GRID_W, NA_WIN_R // 2 * GRID_W)
        kw = k_ref[pl.ds(start, NA_KROWS * GRID_W), :]
        vw = v_ref[pl.ds(start, NA_KROWS * GRID_W), :]
    nt = (((1,), (1,)), ((), ()))
    outs = []
    for hh in range(2):
        sel = (lane < NA_D) if hh == 0 else (lane >= NA_D)
        qz = jnp.where(sel, qf, 0.0).astype(BF16)
        s_c = lax.dot_general(qz, kc, nt, preferred_element_type=F32)
        m = jnp.max(s_c, axis=-1, keepdims=True)
        if lat:
            s_l = lax.dot_general(qz, kw, nt, preferred_element_type=F32) + bias_ref[0, hh]
            m = jnp.maximum(m, jnp.max(s_l, axis=-1, keepdims=True))
        p_c = jnp.exp(s_c - m)
        l = jnp.sum(p_c, axis=-1, keepdims=True)
        o = jnp.dot(p_c.astype(BF16), vc, preferred_element_type=F32)
        if lat:
            p_l = jnp.exp(s_l - m)
            l = l + jnp.sum(p_l, axis=-1, keepdims=True)
            o = o + jnp.dot(p_l.astype(BF16), vw, preferred_element_type=F32)
        outs.append(o / l)
    o_ref[...] = jnp.where(lane < NA_D, outs[0], outs[1]).astype(BF16)


def _na_bias_table(rpb, rows):
    win_r = NA_WIN_R
    hi = lax.Precision.HIGHEST
    cidx = jnp.arange(GRID_W)
    cstart = jnp.clip(cidx - NA_WIN_C // 2, 0, GRID_W - NA_WIN_C)
    col_ok = (cidx[None, :] >= cstart[:, None]) & (cidx[None, :] < cstart[:, None] + NA_WIN_C)
    dc = cidx[None, :] - cidx[:, None] + NA_WIN_C - 1
    oh_c = (dc[:, :, None] == jnp.arange(2 * NA_WIN_C - 1)).astype(F32)
    oh_r, row_ok = [], []
    for r_b in (0, NA_QROWS, rows - NA_QROWS):
        kw0 = min(max(r_b - win_r // 2, 0), rows - NA_KROWS)
        qr = r_b + jnp.arange(NA_QROWS)
        r0 = jnp.clip(qr - win_r // 2, 0, rows - win_r)
        kr = kw0 + jnp.arange(NA_KROWS)
        row_ok.append((kr[None, :] >= r0[:, None]) & (kr[None, :] < r0[:, None] + win_r))
        dr = kr[None, :] - qr[:, None] + win_r - 1
        oh_r.append((dr[:, :, None] == jnp.arange(2 * win_r - 1)).astype(F32))
    oh_r, row_ok = jnp.stack(oh_r), jnp.stack(row_ok)
    tc = jnp.einsum('lhrc,xyc->lhrxy', rpb.astype(F32), oh_c, precision=hi)
    t = jnp.einsum('pqkr,lhrxy->lphqxky', oh_r, tc, precision=hi)
    ok = row_ok[:, :, None, :, None] & col_ok[None, None, :, None, :]
    t = jnp.where(ok[None, :, None], t, NEG_INF)
    return t.reshape(rpb.shape[0], 3, rpb.shape[1], NA_QROWS * GRID_W, NA_KROWS * GRID_W)


def _na_lat_call(p, pc, bias, batch, seq, ctx_len):
    rows = seq // GRID_W
    tq = NA_QROWS * GRID_W
    nblk = seq // tq

    def pat(i):
        return jnp.where(i == 0, 0, jnp.where(i == nblk - 1, 2, 1))

    return pl.pallas_call(
        functools.partial(_na_kernel, rows=rows, lat=True),
        grid=(batch, NA_HEADS // 2, nblk),
        in_specs=[
            pl.BlockSpec((tq, LANES), lambda b, h, i: (b * nblk + i, C_NQ + h)),
            pl.BlockSpec((ctx_len, LANES), lambda b, h, i: (b, C_NK + h)),
            pl.BlockSpec((ctx_len, LANES), lambda b, h, i: (b, C_NV + h)),
            pl.BlockSpec((seq, LANES), lambda b, h, i: (b, C_NK + h)),
            pl.BlockSpec((seq, LANES), lambda b, h, i: (b, C_NV + h)),
            pl.BlockSpec((1, 2, tq, NA_KROWS * GRID_W), lambda b, h, i: (pat(i), h, 0, 0)),
        ],
        out_specs=pl.BlockSpec((tq, LANES), lambda b, h, i: (b * nblk + i, h)),
        out_shape=jax.ShapeDtypeStruct((batch * seq, BRANCH_W), BF16),
        compiler_params=_params("parallel", "parallel", "arbitrary"),
        name="na_attn_lat",
    )(p, pc, pc, p, p, bias)


def _na_ctx_call(pc, batch, ctx_len):
    return pl.pallas_call(
        functools.partial(_na_kernel, rows=0, lat=False),
        grid=(batch, NA_HEADS // 2),
        in_specs=[
            pl.BlockSpec((ctx_len, LANES), lambda b, h: (b, C_NQ + h)),
            pl.BlockSpec((ctx_len, LANES), lambda b, h: (b, C_NK + h)),
            pl.BlockSpec((ctx_len, LANES), lambda b, h: (b, C_NV + h)),
        ],
        out_specs=pl.BlockSpec((ctx_len, LANES), lambda b, h: (b, h)),
        out_shape=jax.ShapeDtypeStruct((batch * ctx_len, BRANCH_W), BF16),
        compiler_params=_params("parallel", "parallel"),
        name="na_attn_ctx",
    )(pc, pc, pc)


def _merge_kernel(x_ref, mod_ref, g0, g1, g2, g3, y0, y1, y2, y3, wb_ref, bb_ref, wo_ref, lng, lnb, o_ref, *, alpha):
    merged = None
    for idx, (g_ref, y_ref) in enumerate(((g0, y0), (g1, y1), (g2, y2), (g3, y3))):
        br = jnp.dot(y_ref[...], wb_ref[idx], preferred_element_type=F32) + bb_ref[idx:idx + 1, :]
        term = _sigmoid(g_ref[...].astype(F32)) * br
        merged = term if merged is None else merged + term
    y = jnp.dot(merged.astype(BF16), wo_ref[...], preferred_element_type=F32)
    gate = mod_ref[0, 2:3, :]
    o_ref[...] = _layer_norm(alpha * x_ref[...] + gate * y, lng[...], lnb[...])


def _merge_call(x, mod, p, ys, wb, bb, wo, lng, lnb, seq, tm, alpha):
    m, d = x.shape
    tpm = seq // tm
    in_specs = ([pl.BlockSpec((tm, d), lambda i: (i, 0)),
                 pl.BlockSpec((1, 6, d), lambda i: (i // tpm, 0, 0))]
                + [pl.BlockSpec((tm, d), functools.partial(lambda i, k: (i, k), k=k)) for k in range(N_BRANCH)]
                + [pl.BlockSpec((tm, BRANCH_W), lambda i: (i, 0))] * N_BRANCH
                + [pl.BlockSpec(wb.shape, lambda i: (0, 0, 0)),
                   pl.BlockSpec(bb.shape, lambda i: (0, 0)),
                   pl.BlockSpec(wo.shape, lambda i: (0, 0)),
                   pl.BlockSpec((1, d), lambda i: (0, 0)),
                   pl.BlockSpec((1, d), lambda i: (0, 0))])
    return pl.pallas_call(
        functools.partial(_merge_kernel, alpha=alpha),
        grid=(m // tm,),
        in_specs=in_specs,
        out_specs=pl.BlockSpec((tm, d), lambda i: (i, 0)),
        out_shape=jax.ShapeDtypeStruct((m, d), F32),
        compiler_params=_params("parallel"),
        name="merge_out",
    )(x, mod, p, p, p, p, *ys, wb, bb, wo, lng.reshape(1, d), lnb.reshape(1, d))


def _ffn_kernel(x_ref, mod_ref, w1_ref, w2_ref, lng, lnb, o_ref, h_ref, acc_ref, *, alpha):
    k = pl.program_id(1)

    @pl.when(k == 0)
    def _():
        sh = mod_ref[0, 3:4, :]
        sc = mod_ref[0, 4:5, :]
        h_ref[...] = (x_ref[...] * (1.0 + sc) + sh).astype(BF16)
        acc_ref[...] = jnp.zeros(acc_ref.shape, F32)

    a = jnp.maximum(jnp.dot(h_ref[...], w1_ref[...], preferred_element_type=F32), 0.0)
    acc_ref[...] += jnp.dot((a * a).astype(BF16), w2_ref[...], preferred_element_type=F32)

    @pl.when(k == pl.num_programs(1) - 1)
    def _():
        gate = mod_ref[0, 5:6, :]
        o_ref[...] = _layer_norm(alpha * x_ref[...] + gate * acc_ref[...], lng[...], lnb[...])


def _ffn_call(x, mod, w1, w2, lng, lnb, seq, tm, kf, alpha):
    m, d = x.shape
    tpm = seq // tm
    return pl.pallas_call(
        functools.partial(_ffn_kernel, alpha=alpha),
        grid=(m // tm, D_FF // kf),
        in_specs=[
            pl.BlockSpec((tm, d), lambda i, k: (i, 0)),
            pl.BlockSpec((1, 6, d), lambda i, k: (i // tpm, 0, 0)),
            pl.BlockSpec((d, kf), lambda i, k: (0, k)),
            pl.BlockSpec((kf, d), lambda i, k: (k, 0)),
            pl.BlockSpec((1, d), lambda i, k: (0, 0)),
            pl.BlockSpec((1, d), lambda i, k: (0, 0)),
        ],
        out_specs=pl.BlockSpec((tm, d), lambda i, k: (i, 0)),
        out_shape=jax.ShapeDtypeStruct((m, d), F32),
        scratch_shapes=[pltpu.VMEM((tm, d), BF16), pltpu.VMEM((tm, d), F32)],
        compiler_params=_params("parallel", "arbitrary"),
        name="ffn",
    )(x, mod, w1, w2, lng.reshape(1, d), lnb.reshape(1, d))


def _rope_tables(seq):
    nf = DIFF_D // 4
    t = jnp.arange(seq)
    freqs = jnp.power(ROPE_BASE, -jnp.arange(nf, dtype=F32) / nf)
    pos = jnp.stack([t // GRID_W, t % GRID_W], axis=-1).astype(F32)
    ang = pos[:, :, None] * freqs
    cos, sin = jnp.cos(ang), jnp.sin(ang)
    shape = (seq, LANES // DIFF_D, 2, 2, nf)
    cos_t = jnp.broadcast_to(cos[:, None, :, None, :], shape)
    sin_t = jnp.broadcast_to(sin[:, None, :, None, :], shape)
    half = jnp.arange(2)[None, None, None, :, None]
    sin_m = jnp.where(half == 0, -sin_t, 0.0)
    sin_p = jnp.where(half == 1, sin_t, 0.0)
    return tuple(a.reshape(seq, LANES) for a in (cos_t, sin_m, sin_p))


def kernel(x, c, ctx, c_ctx, w_ada, b_ada, w_in, b_in, conv_a_w, conv_a_b, conf_dw_w, conf_dw_b, conf_ln_g, conf_ln_b, diff_lambda, diff_norm_g, na_rpb, w_branch, b_branch, w_o, ln_g, ln_b, w_ff1, w_ff2):
    batch, seq, d = x.shape
    ctx_len = ctx.shape[1]
    depth = w_ada.shape[0]
    rows = seq // GRID_W
    assert d == D_MODEL and seq % (NA_QROWS * GRID_W) == 0 and rows >= NA_KROWS
    alpha = (2 * depth) ** 0.25

    cond_rows = -(-(batch + 1) // 8) * 8
    cond = jnp.zeros((cond_rows, d), F32).at[:batch].set(c).at[batch].set(c_ctx)
    mods = _ada_call(cond, w_ada, b_ada)

    w_in_p = jnp.concatenate([w_in[:, :, MAIN_W:], w_in[:, :, :MAIN_W]], axis=-1).astype(BF16)
    b_in_p = jnp.concatenate([b_in[:, MAIN_W:], b_in[:, :MAIN_W]], axis=-1).reshape(depth, 1, PROJ_W)
    w_branch_b = w_branch.astype(BF16)
    w_o_b = w_o.astype(BF16)
    w_ff1_b = w_ff1.astype(BF16)
    w_ff2_b = w_ff2.astype(BF16)
    rope_tabs = _rope_tables(seq)

    tm_lat = min(1024, seq)
    tq = min(256, seq)
    tk = min(1024, seq)
    na_bias = _na_bias_table(na_rpb, rows)
    m_ctx = batch * ctx_len

    xl = x.reshape(batch * seq, d)
    xc = ctx.reshape(m_ctx, d)
    for l in range(depth):
        last = l == depth - 1
        lam_init = 0.8 - 0.6 * math.exp(-0.3 * l)
        mod_lat = mods[l, :batch].reshape(batch, 6, d)
        mod_ctx = mods[l, batch:batch + 1].reshape(1, 6, d)
        local_w = (conv_a_w[l], conv_a_b[l], conf_dw_w[l], conf_dw_b[l], conf_ln_g[l], conf_ln_b[l])

        pc = _proj_call(xc, mod_ctx, w_in_p[l], b_in_p[l], None, m_ctx, m_ctx)
        p = _proj_call(xl, mod_lat, w_in_p[l], b_in_p[l], rope_tabs, seq, tm_lat)

        y_a, y_b = _local_call(p, *local_w, seq, min(512, seq))
        y_c = _diff_call(diff_lambda[l], p, pc, p, diff_norm_g[l], lam_init, batch, seq, ctx_len, seq, tq, tk)
        y_d = _na_lat_call(p, pc, na_bias[l], batch, seq, ctx_len)
        x1 = _merge_call(xl, mod_lat, p, (y_a, y_b, y_c, y_d), w_branch_b[l], b_branch[l], w_o_b[l],
                         ln_g[l, 0], ln_b[l, 0], seq, min(512, seq), alpha)
        xl = _ffn_call(x1, mod_lat, w_ff1_b[l], w_ff2_b[l], ln_g[l, 1], ln_b[l, 1], seq, tm_lat, 512, alpha)

        if not last:
            yc_a, yc_b = _local_call(pc, *local_w, ctx_len, ctx_len)
            yc_c = _diff_call(diff_lambda[l], pc, pc, None, diff_norm_g[l], lam_init, batch, ctx_len, ctx_len, 0,
                              ctx_len, tk)
            yc_d = _na_ctx_call(pc, batch, ctx_len)
            xc1 = _merge_call(xc, mod_ctx, pc, (yc_a, yc_b, yc_c, yc_d), w_branch_b[l], b_branch[l], w_o_b[l],
                              ln_g[l, 0], ln_b[l, 0], m_ctx, min(512, m_ctx), alpha)
            xc = _ffn_call(xc1, mod_ctx, w_ff1_b[l], w_ff2_b[l], ln_g[l, 1], ln_b[l, 1], m_ctx, min(1024, m_ctx),
                           512, alpha)
    return xl.reshape(batch, seq, d)
```

```python
import functools
import math

import jax
import jax.numpy as jnp
from jax import lax
from jax.experimental import pallas as pl
from jax.experimental.pallas import tpu as pltpu

F32 = jnp.float32
BF16 = jnp.bfloat16

D_MODEL = 1024
GRID_W = 64
BRANCH_W = D_MODEL // 2
N_BRANCH = 4
CONV_A_K = 3
CONF_K = 31
DIFF_HEADS = 4
DIFF_D = D_MODEL // 16
DIFF_V = 2 * DIFF_D
NA_HEADS = 8
NA_D = D_MODEL // 16
NA_WIN_R = 8
NA_WIN_C = 16
D_FF = 4 * D_MODEL
ROPE_BASE = 10000.0
LN_EPS = 1e-5
NEG_INF = -1e30

GATE_W = N_BRANCH * D_MODEL
MAIN_W = 11 * BRANCH_W
PROJ_W = GATE_W + MAIN_W
PROJ_TN = 512
T_GATE_END = GATE_W // PROJ_TN
T_AB, T_AC, T_AX, T_CA, T_CG, T_DQ, T_DK, T_DV, T_NQ, T_NK, T_NV = range(T_GATE_END, T_GATE_END + 11)
LANES = 128
C_DQ, C_DK, C_DV, C_NQ, C_NK, C_NV = (t * (PROJ_TN // LANES) for t in (T_DQ, T_DK, T_DV, T_NQ, T_NK, T_NV))

HALO = 16
NA_QROWS = 8
NA_KROWS = 16
VMEM_LIMIT = 56 * 1024 * 1024
Q_SCALE = DIFF_D ** -0.5 * math.log2(math.e)


def _params(*sem):
    return pltpu.CompilerParams(dimension_semantics=sem, vmem_limit_bytes=VMEM_LIMIT)


def _sigmoid(v):
    return 1.0 / (1.0 + jnp.exp(-v))


def _layer_norm(v, g, b):
    mu = jnp.mean(v, axis=-1, keepdims=True)
    d = v - mu
    var = jnp.mean(d * d, axis=-1, keepdims=True)
    return d * lax.rsqrt(var + LN_EPS) * g + b


def _ada_kernel(cond_ref, w_ref, b_ref, o_ref):
    cnd = cond_ref[...]
    act = (cnd * _sigmoid(cnd)).astype(BF16)
    o_ref[0] = jnp.dot(act, w_ref[0].astype(BF16), preferred_element_type=F32) + b_ref[0]


def _ada_call(cond, w_ada, b_ada):
    depth, d, n = w_ada.shape
    rows = cond.shape[0]
    tn = 1024
    return pl.pallas_call(
        _ada_kernel,
        grid=(depth, n // tn),
        in_specs=[
            pl.BlockSpec((rows, d), lambda l, j: (0, 0)),
            pl.BlockSpec((1, d, tn), lambda l, j: (l, 0, j)),
            pl.BlockSpec((1, 1, tn), lambda l, j: (l, 0, j)),
        ],
        out_specs=pl.BlockSpec((1, rows, tn), lambda l, j: (l, 0, j)),
        out_shape=jax.ShapeDtypeStruct((depth, rows, n), F32),
        compiler_params=_params("parallel", "parallel"),
        name="ada_mod",
    )(cond, w_ada, b_ada.reshape(depth, 1, n))


def _proj_kernel(*refs, rope):
    if rope:
        x_ref, mod_ref, w_ref, b_ref, cos_ref, sinm_ref, sinp_ref, o_ref, u_ref = refs
    else:
        x_ref, mod_ref, w_ref, b_ref, o_ref, u_ref = refs
    n = pl.program_id(1)

    @pl.when(n == 0)
    def _():
        sh = mod_ref[0, 0:1, :]
        sc = mod_ref[0, 1:2, :]
        u_ref[...] = (x_ref[...] * (1.0 + sc) + sh).astype(BF16)

    acc = jnp.dot(u_ref[...], w_ref[...], preferred_element_type=F32) + b_ref[...]
    is_q = jnp.logical_or(n == T_DQ, n == T_NQ)
    scale = jnp.where(is_q, Q_SCALE, 1.0).astype(F32)
    if rope:
        is_rope = jnp.logical_or(n == T_DQ, n == T_DK)

        @pl.when(is_rope)
        def _():
            cs = cos_ref[...]
            sm = sinm_ref[...]
            sp = sinp_ref[...]
            for g in range(PROJ_TN // LANES):
                xg = acc[:, g * LANES:(g + 1) * LANES]
                rot = xg * cs + pltpu.roll(xg, LANES - DIFF_D // 4, 1) * sm + pltpu.roll(xg, DIFF_D // 4, 1) * sp
                o_ref[:, g * LANES:(g + 1) * LANES] = (rot * scale).astype(BF16)

        @pl.when(jnp.logical_not(is_rope))
        def _():
            o_ref[...] = (acc * scale).astype(BF16)
    else:
        o_ref[...] = (acc * scale).astype(BF16)


def _proj_call(x, mod, w, b, rope_tabs, seq, tm):
    m, d = x.shape
    ntile = PROJ_W // PROJ_TN
    tpm = seq // tm
    rope = rope_tabs is not None
    in_specs = [
        pl.BlockSpec((tm, d), lambda i, n: (i, 0)),
        pl.BlockSpec((1, 6, d), lambda i, n: (i // tpm, 0, 0)),
        pl.BlockSpec((d, PROJ_TN), lambda i, n: (0, n)),
        pl.BlockSpec((1, PROJ_TN), lambda i, n: (0, n)),
    ]
    args = [x, mod, w, b]
    if rope:
        in_specs += [pl.BlockSpec((tm, LANES), lambda i, n: (i % tpm, 0))] * 3
        args += list(rope_tabs)
    return pl.pallas_call(
        functools.partial(_proj_kernel, rope=rope),
        grid=(m // tm, ntile),
        in_specs=in_specs,
        out_specs=pl.BlockSpec((tm, PROJ_TN), lambda i, n: (i, n)),
        out_shape=jax.ShapeDtypeStruct((m, PROJ_W), BF16),
        scratch_shapes=[pltpu.VMEM((tm, d), BF16)],
        compiler_params=_params("parallel", "arbitrary"),
        name="proj_lat" if rope else "proj_ctx",
    )(*args)


def _local_kernel(ab, ac, ax, ca, cg, acp, axp, cap, cgp, acn, axn, can, cgn,
                  caw, cab, dww, dwb, lng, lnb, ya_ref, yb_ref, z_ref, glu_ref, conv_ref, *, tt, tps):
    i = pl.program_id(0)
    si = i % tps
    has_prev = (si > 0).astype(F32)
    has_next = (si < tps - 1).astype(F32)

    def glu(a, g):
        return a[...].astype(F32) * _sigmoid(g[...].astype(F32))

    def prod(a, b):
        return a[...].astype(F32) * b[...].astype(F32)

    glu_ref[0:HALO, :] = glu(cap, cgp) * has_prev
    glu_ref[HALO:HALO + tt, :] = glu(ca, cg)
    glu_ref[HALO + tt:, :] = glu(can, cgn) * has_next
    z_ref[0:HALO, :] = prod(acp, axp) * has_prev
    z_ref[HALO:HALO + tt, :] = prod(ac, ax)
    z_ref[HALO + tt:, :] = prod(acn, axn) * has_next

    rc = min(tt, 128)
    for g in range(BRANCH_W // LANES):
        ls = slice(g * LANES, (g + 1) * LANES)
        for r0 in range(0, tt, rc):
            acc = z_ref[pl.ds(r0 + HALO - 1, rc), ls] * caw[0:1, ls]
            for k in range(1, CONV_A_K):
                acc = acc + z_ref[pl.ds(r0 + HALO - 1 + k, rc), ls] * caw[k:k + 1, ls]
            ya = ab[pl.ds(r0, rc), ls].astype(F32) * (acc + cab[:, ls])
            ya_ref[pl.ds(r0, rc), ls] = ya.astype(BF16)

            off = HALO - CONF_K // 2
            acc = glu_ref[pl.ds(r0 + off, rc), ls] * dww[0:1, ls]
            for k in range(1, CONF_K):
                acc = acc + glu_ref[pl.ds(r0 + off + k, rc), ls] * dww[k:k + 1, ls]
            conv_ref[pl.ds(r0, rc), ls] = acc + dwb[:, ls]

    yn = _layer_norm(conv_ref[...], lng[...], lnb[...])
    yb_ref[...] = (yn * _sigmoid(yn)).astype(BF16)


def _local_call(p, caw, cab, dww, dwb, lng, lnb, seq, tt):
    m = p.shape[0]
    tps = seq // tt
    hb = tt // HALO
    last = m // HALO - 1

    def cur(t):
        return pl.BlockSpec((tt, BRANCH_W), lambda i: (i, t))

    def prev(t):
        return pl.BlockSpec((HALO, BRANCH_W), lambda i: (jnp.maximum(i * hb - 1, 0), t))

    def nxt(t):
        return pl.BlockSpec((HALO, BRANCH_W), lambda i: (jnp.minimum((i + 1) * hb, last), t))

    def full(a):
        return pl.BlockSpec(a.shape, lambda i: (0,) * a.ndim)

    small = [caw, cab.reshape(1, -1), dww, dwb.reshape(1, -1), lng.reshape(1, -1), lnb.reshape(1, -1)]
    in_specs = ([cur(t) for t in (T_AB, T_AC, T_AX, T_CA, T_CG)]
                + [prev(t) for t in (T_AC, T_AX, T_CA, T_CG)]
                + [nxt(t) for t in (T_AC, T_AX, T_CA, T_CG)]
                + [full(a) for a in small])
    out = jax.ShapeDtypeStruct((m, BRANCH_W), BF16)
    return pl.pallas_call(
        functools.partial(_local_kernel, tt=tt, tps=tps),
        grid=(m // tt,),
        in_specs=in_specs,
        out_specs=[pl.BlockSpec((tt, BRANCH_W), lambda i: (i, 0))] * 2,
        out_shape=[out, out],
        scratch_shapes=[pltpu.VMEM((tt + 2 * HALO, BRANCH_W), F32),
                        pltpu.VMEM((tt + 2 * HALO, BRANCH_W), F32),
                        pltpu.VMEM((tt, BRANCH_W), F32)],
        compiler_params=_params("parallel"),
        name="local_branches",
    )(*([p] * 13), *small)


def _diff_kernel(*refs, tq, tk, n_lat, lam_init, unroll):
    if n_lat:
        dl_ref, q_ref, kc_ref, vc_ref, k_ref, v_ref, g_ref, o_ref, m_ref, acc_ref, vca_ref, va_ref = refs
    else:
        dl_ref, q_ref, kc_ref, vc_ref, g_ref, o_ref, m_ref, acc_ref, vca_ref = refs
    dl = dl_ref[...]
    lam = (jnp.exp(jnp.sum(dl[0:1] * dl[1:2], axis=-1, keepdims=True))
           - jnp.exp(jnp.sum(dl[2:3] * dl[3:4], axis=-1, keepdims=True)) + lam_init)

    @pl.when(pl.program_id(2) == 0)
    def _():
        vca_ref[:, :DIFF_V] = vc_ref[...]
        vca_ref[:, DIFF_V:] = jnp.ones((vca_ref.shape[0], DIFF_V), BF16)
        if n_lat:
            va_ref[:, :DIFF_V] = v_ref[...]
            va_ref[:, DIFF_V:] = jnp.ones((va_ref.shape[0], DIFF_V), BF16)

    qf = q_ref[...].astype(F32)
    lane = lax.broadcasted_iota(jnp.int32, qf.shape, 1)
    q2 = jnp.concatenate([jnp.where(lane < DIFF_D, qf, 0.0), jnp.where(lane >= DIFF_D, qf, 0.0)],
                         axis=0).astype(BF16)

    m_ref[...] = jnp.full(m_ref.shape, NEG_INF, F32)
    acc_ref[...] = jnp.zeros(acc_ref.shape, F32)

    def step(kb, vb):
        nc = kb.shape[0] // LANES
        s = lax.dot_general(q2, kb, (((1,), (1,)), ((), ())), preferred_element_type=F32)
        cols = [s[:, c * LANES:(c + 1) * LANES] for c in range(nc)]
        cmax = cols[0]
        for sc in cols[1:]:
            cmax = jnp.maximum(cmax, sc)
        m_old = m_ref[...]
        m_new = jnp.maximum(m_old, jnp.max(cmax, axis=1, keepdims=True))
        alpha = jnp.exp2(m_old - m_new)
        p = jnp.concatenate([jnp.exp2(sc - m_new) for sc in cols], axis=1).astype(BF16)
        pv = jnp.dot(p, vb, preferred_element_type=F32)
        acc_ref[...] = jnp.concatenate([alpha, alpha], axis=1) * acc_ref[...] + pv
        m_ref[...] = m_new

    step(kc_ref[...], vca_ref[...])
    if n_lat:
        def body(j, carry):
            j0 = pl.multiple_of(j * tk, tk)
            step(k_ref[pl.ds(j0, tk), :], va_ref[pl.ds(j0, tk), :])
            return carry
        lax.fori_loop(0, n_lat // tk, body, 0, unroll=unroll)

    acc = acc_ref[...]
    o = acc[:, :DIFF_V] / acc[:, DIFF_V:]
    od = o[:tq] - lam * o[tq:]
    ms = jnp.mean(od * od, axis=-1, keepdims=True)
    o_ref[...] = (od * lax.rsqrt(ms + LN_EPS) * g_ref[...] * (1.0 - lam_init)).astype(BF16)


def _diff_call(dl, pq, pc, pk, g, lam_init, batch, q_seq, ctx_len, lat_seq, tq, tk):
    nq = q_seq // tq
    in_specs = [
        pl.BlockSpec(dl.shape, lambda b, h, i: (0, 0)),
        pl.BlockSpec((tq, LANES), lambda b, h, i: (b * nq + i, C_DQ + h)),
        pl.BlockSpec((ctx_len, LANES), lambda b, h, i: (b, C_DK + h)),
        pl.BlockSpec((ctx_len, LANES), lambda b, h, i: (b, C_DV + h)),
    ]
    args = [dl, pq, pc, pc]
    if pk is not None:
        in_specs += [pl.BlockSpec((lat_seq, LANES), lambda b, h, i: (b, C_DK + h)),
                     pl.BlockSpec((lat_seq, LANES), lambda b, h, i: (b, C_DV + h))]
        args += [pk, pk]
    in_specs.append(pl.BlockSpec((1, DIFF_V), lambda b, h, i: (0, 0)))
    args.append(g.reshape(1, DIFF_V))
    return pl.pallas_call(
        functools.partial(_diff_kernel, tq=tq, tk=tk, n_lat=lat_seq if pk is not None else 0, lam_init=lam_init,
                          unroll=8),
        grid=(batch, DIFF_HEADS, nq),
        in_specs=in_specs,
        out_specs=pl.BlockSpec((tq, DIFF_V), lambda b, h, i: (b * nq + i, h)),
        out_shape=jax.ShapeDtypeStruct((batch * q_seq, BRANCH_W), BF16),
        scratch_shapes=([pltpu.VMEM((2 * tq, LANES), F32), pltpu.VMEM((2 * tq, 2 * DIFF_V), F32),
                         pltpu.VMEM((ctx_len, 2 * DIFF_V), BF16)]
                        + ([pltpu.VMEM((lat_seq, 2 * DIFF_V), BF16)] if pk is not None else [])),
        compiler_params=_params("arbitrary", "arbitrary", "arbitrary"),
        name="diff_attn_lat" if pk is not None else "diff_attn_ctx",
    )(*args)


def _na_kernel(*refs, rows, lat):
    if lat:
        q_ref, kc_ref, vc_ref, k_ref, v_ref, bias_ref, o_ref = refs
    else:
        q_ref, kc_ref, vc_ref, o_ref = refs
    qf = q_ref[...].astype(F32)
    lane = lax.broadcasted_iota(jnp.int32, qf.shape, 1)
    kc = kc_ref[...]
    vc = vc_ref[...]
    if lat:
        i = pl.program_id(2)
        kw0 = jnp.clip(i * NA_QROWS - NA_WIN_R // 2, 0, rows - NA_KROWS)
        start = pl.multiple_of(kw0 * GRID_W, NA_WIN_R // 2 * GRID_W)
        kw = k_ref[pl.ds(start, NA_KROWS * GRID_W), :]
        vw = v_ref[pl.ds(start, NA_KROWS * GRID_W), :]
    nt = (((1,), (1,)), ((), ()))
    outs = []
    for hh in range(2):
        sel = (lane < NA_D) if hh == 0 else (lane >= NA_D)
        qz = jnp.where(sel, qf, 0.0).astype(BF16)
        s_c = lax.dot_general(qz, kc, nt, preferred_element_type=F32)
        m = jnp.max(s_c, axis=-1, keepdims=True)
        if lat:
            s_l = lax.dot_general(qz, kw, nt, preferred_element_type=F32) + bias_ref[0, hh]
            m = jnp.maximum(m, jnp.max(s_l, axis=-1, keepdims=True))
        p_c = jnp.exp2(s_c - m)
        l = jnp.sum(p_c, axis=-1, keepdims=True)
        o = jnp.dot(p_c.astype(BF16), vc, preferred_element_type=F32)
        if lat:
            p_l = jnp.exp2(s_l - m)
            l = l + jnp.sum(p_l, axis=-1, keepdims=True)
            o = o + jnp.dot(p_l.astype(BF16), vw, preferred_element_type=F32)
        outs.append(o / l)
    o_ref[...] = jnp.where(lane < NA_D, outs[0], outs[1]).astype(BF16)


def _na_bias_table(rpb, rows):
    win_r = NA_WIN_R
    hi = lax.Precision.HIGHEST
    cidx = jnp.arange(GRID_W)
    cstart = jnp.clip(cidx - NA_WIN_C // 2, 0, GRID_W - NA_WIN_C)
    col_ok = (cidx[None, :] >= cstart[:, None]) & (cidx[None, :] < cstart[:, None] + NA_WIN_C)
    dc = cidx[None, :] - cidx[:, None] + NA_WIN_C - 1
    oh_c = (dc[:, :, None] == jnp.arange(2 * NA_WIN_C - 1)).astype(F32)
    oh_r, row_ok = [], []
    for r_b in (0, NA_QROWS, rows - NA_QROWS):
        kw0 = min(max(r_b - win_r // 2, 0), rows - NA_KROWS)
        qr = r_b + jnp.arange(NA_QROWS)
        r0 = jnp.clip(qr - win_r // 2, 0, rows - win_r)
        kr = kw0 + jnp.arange(NA_KROWS)
        row_ok.append((kr[None, :] >= r0[:, None]) & (kr[None, :] < r0[:, None] + win_r))
        dr = kr[None, :] - qr[:, None] + win_r - 1
        oh_r.append((dr[:, :, None] == jnp.arange(2 * win_r - 1)).astype(F32))
    oh_r, row_ok = jnp.stack(oh_r), jnp.stack(row_ok)
    tc = jnp.einsum('lhrc,xyc->lhrxy', rpb.astype(F32), oh_c, precision=hi)
    t = jnp.einsum('pqkr,lhrxy->lphqxky', oh_r, tc, precision=hi)
    ok = row_ok[:, :, None, :, None] & col_ok[None, None, :, None, :]
    t = jnp.where(ok[None, :, None], t * math.log2(math.e), NEG_INF)
    return t.reshape(rpb.shape[0], 3, rpb.shape[1], NA_QROWS * GRID_W, NA_KROWS * GRID_W)


def _na_lat_call(p, pc, bias, batch, seq, ctx_len):
    rows = seq // GRID_W
    tq = NA_QROWS * GRID_W
    nblk = seq // tq

    def pat(i):
        return jnp.where(i == 0, 0, jnp.where(i == nblk - 1, 2, 1))

    return pl.pallas_call(
        functools.partial(_na_kernel, rows=rows, lat=True),
        grid=(batch, NA_HEADS // 2, nblk),
        in_specs=[
            pl.BlockSpec((tq, LANES), lambda b, h, i: (b * nblk + i, C_NQ + h)),
            pl.BlockSpec((ctx_len, LANES), lambda b, h, i: (b, C_NK + h)),
            pl.BlockSpec((ctx_len, LANES), lambda b, h, i: (b, C_NV + h)),
            pl.BlockSpec((seq, LANES), lambda b, h, i: (b, C_NK + h)),
            pl.BlockSpec((seq, LANES), lambda b, h, i: (b, C_NV + h)),
            pl.BlockSpec((1, 2, tq, NA_KROWS * GRID_W), lambda b, h, i: (pat(i), h, 0, 0)),
        ],
        out_specs=pl.BlockSpec((tq, LANES), lambda b, h, i: (b * nblk + i, h)),
        out_shape=jax.ShapeDtypeStruct((batch * seq, BRANCH_W), BF16),
        compiler_params=_params("parallel", "parallel", "arbitrary"),
        name="na_attn_lat",
    )(p, pc, pc, p, p, bias)


def _na_ctx_call(pc, batch, ctx_len):
    return pl.pallas_call(
        functools.partial(_na_kernel, rows=0, lat=False),
        grid=(batch, NA_HEADS // 2),
        in_specs=[
            pl.BlockSpec((ctx_len, LANES), lambda b, h: (b, C_NQ + h)),
            pl.BlockSpec((ctx_len, LANES), lambda b, h: (b, C_NK + h)),
            pl.BlockSpec((ctx_len, LANES), lambda b, h: (b, C_NV + h)),
        ],
        out_specs=pl.BlockSpec((ctx_len, LANES), lambda b, h: (b, h)),
        out_shape=jax.ShapeDtypeStruct((batch * ctx_len, BRANCH_W), BF16),
        compiler_params=_params("parallel", "parallel"),
        name="na_attn_ctx",
    )(pc, pc, pc)


def _merge_kernel(x_ref, mod_ref, g0, g1, g2, g3, y0, y1, y2, y3, wb_ref, bb_ref, wo_ref, lng, lnb, o_ref, *, alpha):
    merged = None
    for idx, (g_ref, y_ref) in enumerate(((g0, y0), (g1, y1), (g2, y2), (g3, y3))):
        br = jnp.dot(y_ref[...], wb_ref[idx], preferred_element_type=F32) + bb_ref[idx:idx + 1, :]
        term = _sigmoid(g_ref[...].astype(F32)) * br
        merged = term if merged is None else merged + term
    y = jnp.dot(merged.astype(BF16), wo_ref[...], preferred_element_type=F32)
    gate = mod_ref[0, 2:3, :]
    o_ref[...] = _layer_norm(alpha * x_ref[...] + gate * y, lng[...], lnb[...])


def _merge_call(x, mod, p, ys, wb, bb, wo, lng, lnb, seq, tm, alpha):
    m, d = x.shape
    tpm = seq // tm
    in_specs = ([pl.BlockSpec((tm, d), lambda i: (i, 0)),
                 pl.BlockSpec((1, 6, d), lambda i: (i // tpm, 0, 0))]
                + [pl.BlockSpec((tm, d), functools.partial(lambda i, k: (i, k), k=k)) for k in range(N_BRANCH)]
                + [pl.BlockSpec((tm, BRANCH_W), lambda i: (i, 0))] * N_BRANCH
                + [pl.BlockSpec(wb.shape, lambda i: (0, 0, 0)),
                   pl.BlockSpec(bb.shape, lambda i: (0, 0)),
                   pl.BlockSpec(wo.shape, lambda i: (0, 0)),
                   pl.BlockSpec((1, d), lambda i: (0, 0)),
                   pl.BlockSpec((1, d), lambda i: (0, 0))])
    return pl.pallas_call(
        functools.partial(_merge_kernel, alpha=alpha),
        grid=(m // tm,),
        in_specs=in_specs,
        out_specs=pl.BlockSpec((tm, d), lambda i: (i, 0)),
        out_shape=jax.ShapeDtypeStruct((m, d), F32),
        compiler_params=_params("parallel"),
        name="merge_out",
    )(x, mod, p, p, p, p, *ys, wb, bb, wo, lng.reshape(1, d), lnb.reshape(1, d))


def _ffn_kernel(x_ref, mod_ref, w1_ref, w2_ref, lng, lnb, o_ref, h_ref, acc_ref, *, alpha):
    k = pl.program_id(1)

    @pl.when(k == 0)
    def _():
        sh = mod_ref[0, 3:4, :]
        sc = mod_ref[0, 4:5, :]
        h_ref[...] = (x_ref[...] * (1.0 + sc) + sh).astype(BF16)
        acc_ref[...] = jnp.zeros(acc_ref.shape, F32)

    a = jnp.maximum(jnp.dot(h_ref[...], w1_ref[...], preferred_element_type=F32), 0.0)
    acc_ref[...] += jnp.dot((a * a).astype(BF16), w2_ref[...], preferred_element_type=F32)

    @pl.when(k == pl.num_programs(1) - 1)
    def _():
        gate = mod_ref[0, 5:6, :]
        o_ref[...] = _layer_norm(alpha * x_ref[...] + gate * acc_ref[...], lng[...], lnb[...])


def _ffn_call(x, mod, w1, w2, lng, lnb, seq, tm, kf, alpha):
    m, d = x.shape
    tpm = seq // tm
    return pl.pallas_call(
        functools.partial(_ffn_kernel, alpha=alpha),
        grid=(m // tm, D_FF // kf),
        in_specs=[
            pl.BlockSpec((tm, d), lambda i, k: (i, 0)),
            pl.BlockSpec((1, 6, d), lambda i, k: (i // tpm, 0, 0)),
            pl.BlockSpec((d, kf), lambda i, k: (0, k)),
            pl.BlockSpec((kf, d), lambda i, k: (k, 0)),
            pl.BlockSpec((1, d), lambda i, k: (0, 0)),
            pl.BlockSpec((1, d), lambda i, k: (0, 0)),
        ],
        out_specs=pl.BlockSpec((tm, d), lambda i, k: (i, 0)),
        out_shape=jax.ShapeDtypeStruct((m, d), F32),
        scratch_shapes=[pltpu.VMEM((tm, d), BF16), pltpu.VMEM((tm, d), F32)],
        compiler_params=_params("parallel", "arbitrary"),
        name="ffn",
    )(x, mod, w1, w2, lng.reshape(1, d), lnb.reshape(1, d))


def _rope_tables(seq):
    nf = DIFF_D // 4
    t = jnp.arange(seq)
    freqs = jnp.power(ROPE_BASE, -jnp.arange(nf, dtype=F32) / nf)
    pos = jnp.stack([t // GRID_W, t % GRID_W], axis=-1).astype(F32)
    ang = pos[:, :, None] * freqs
    cos, sin = jnp.cos(ang), jnp.sin(ang)
    shape = (seq, LANES // DIFF_D, 2, 2, nf)
    cos_t = jnp.broadcast_to(cos[:, None, :, None, :], shape)
    sin_t = jnp.broadcast_to(sin[:, None, :, None, :], shape)
    half = jnp.arange(2)[None, None, None, :, None]
    sin_m = jnp.where(half == 0, -sin_t, 0.0)
    sin_p = jnp.where(half == 1, sin_t, 0.0)
    return tuple(a.reshape(seq, LANES) for a in (cos_t, sin_m, sin_p))


def kernel(x, c, ctx, c_ctx, w_ada, b_ada, w_in, b_in, conv_a_w, conv_a_b, conf_dw_w, conf_dw_b, conf_ln_g, conf_ln_b, diff_lambda, diff_norm_g, na_rpb, w_branch, b_branch, w_o, ln_g, ln_b, w_ff1, w_ff2):
    batch, seq, d = x.shape
    ctx_len = ctx.shape[1]
    depth = w_ada.shape[0]
    rows = seq // GRID_W
    assert d == D_MODEL and seq % (NA_QROWS * GRID_W) == 0 and rows >= NA_KROWS
    alpha = (2 * depth) ** 0.25

    cond_rows = -(-(batch + 1) // 8) * 8
    cond = jnp.zeros((cond_rows, d), F32).at[:batch].set(c).at[batch].set(c_ctx)
    mods = _ada_call(cond, w_ada, b_ada)

    w_in_p = jnp.concatenate([w_in[:, :, MAIN_W:], w_in[:, :, :MAIN_W]], axis=-1).astype(BF16)
    b_in_p = jnp.concatenate([b_in[:, MAIN_W:], b_in[:, :MAIN_W]], axis=-1).reshape(depth, 1, PROJ_W)
    w_branch_b = w_branch.astype(BF16)
    w_o_b = w_o.astype(BF16)
    w_ff1_b = w_ff1.astype(BF16)
    w_ff2_b = w_ff2.astype(BF16)
    rope_tabs = _rope_tables(seq)

    tm_lat = min(2048, seq)
    tq = min(256, seq)
    tk = min(1024, seq)
    na_bias = _na_bias_table(na_rpb, rows)
    m_ctx = batch * ctx_len

    xl = x.reshape(batch * seq, d)
    xc = ctx.reshape(m_ctx, d)
    for l in range(depth):
        last = l == depth - 1
        lam_init = 0.8 - 0.6 * math.exp(-0.3 * l)
        mod_lat = mods[l, :batch].reshape(batch, 6, d)
        mod_ctx = mods[l, batch:batch + 1].reshape(1, 6, d)
        local_w = (conv_a_w[l], conv_a_b[l], conf_dw_w[l], conf_dw_b[l], conf_ln_g[l], conf_ln_b[l])

        pc = _proj_call(xc, mod_ctx, w_in_p[l], b_in_p[l], None, m_ctx, m_ctx)
        p = _proj_call(xl, mod_lat, w_in_p[l], b_in_p[l], rope_tabs, seq, tm_lat)

        y_a, y_b = _local_call(p, *local_w, seq, min(512, seq))
        y_c = _diff_call(diff_lambda[l], p, pc, p, diff_norm_g[l], lam_init, batch, seq, ctx_len, seq, tq, tk)
        y_d = _na_lat_call(p, pc, na_bias[l], batch, seq, ctx_len)
        x1 = _merge_call(xl, mod_lat, p, (y_a, y_b, y_c, y_d), w_branch_b[l], b_branch[l], w_o_b[l],
                         ln_g[l, 0], ln_b[l, 0], seq, min(512, seq), alpha)
        xl = _ffn_call(x1, mod_lat, w_ff1_b[l], w_ff2_b[l], ln_g[l, 1], ln_b[l, 1], seq, min(1024, seq), 512, alpha)

        if not last:
            yc_a, yc_b = _local_call(pc, *local_w, ctx_len, ctx_len)
            yc_c = _diff_call(diff_lambda[l], pc, pc, None, diff_norm_g[l], lam_init, batch, ctx_len, ctx_len, 0,
                              ctx_len, tk)
            yc_d = _na_ctx_call(pc, batch, ctx_len)
            xc1 = _merge_call(xc, mod_ctx, pc, (yc_a, yc_b, yc_c, yc_d), w_branch_b[l], b_branch[l], w_o_b[l],
                              ln_g[l, 0], ln_b[l, 0], m_ctx, min(512, m_ctx), alpha)
            xc = _ffn_call(xc1, mod_ctx, w_ff1_b[l], w_ff2_b[l], ln_g[l, 1], ln_b[l, 1], m_ctx, min(1024, m_ctx),
                           512, alpha)
    return xl.reshape(batch, seq, d)
```

```python
import functools
import math

import jax
import jax.numpy as jnp
from jax import lax
from jax.experimental import pallas as pl
from jax.experimental.pallas import tpu as pltpu

F32 = jnp.float32
BF16 = jnp.bfloat16

D_MODEL = 1024
GRID_W = 64
BRANCH_W = D_MODEL // 2
N_BRANCH = 4
CONV_A_K = 3
CONF_K = 31
DIFF_HEADS = 4
DIFF_D = D_MODEL // 16
DIFF_V = 2 * DIFF_D
NA_HEADS = 8
NA_D = D_MODEL // 16
NA_WIN_R = 8
NA_WIN_C = 16
D_FF = 4 * D_MODEL
ROPE_BASE = 10000.0
LN_EPS = 1e-5
NEG_INF = -1e30

GATE_W = N_BRANCH * D_MODEL
MAIN_W = 11 * BRANCH_W
PROJ_W = GATE_W + MAIN_W
PROJ_TN = 512
T_GATE_END = GATE_W // PROJ_TN
T_AB, T_AC, T_AX, T_CA, T_CG, T_DQ, T_DK, T_DV, T_NQ, T_NK, T_NV = range(T_GATE_END, T_GATE_END + 11)
LANES = 128
C_DQ, C_DK, C_DV, C_NQ, C_NK, C_NV = (t * (PROJ_TN // LANES) for t in (T_DQ, T_DK, T_DV, T_NQ, T_NK, T_NV))

HALO = 16
SUBLANES = 8
Z_SHIFTS = tuple(sorted({(HALO - CONV_A_K // 2 + k) % SUBLANES for k in range(CONV_A_K)}))
NA_QROWS = 8
NA_KROWS = 16
NA_HPS = 4
VMEM_LIMIT = 56 * 1024 * 1024
Q_SCALE = DIFF_D ** -0.5 * math.log2(math.e)


def _params(*sem):
    return pltpu.CompilerParams(dimension_semantics=sem, vmem_limit_bytes=VMEM_LIMIT)


def _sigmoid(v):
    return 1.0 / (1.0 + jnp.exp(-v))


def _layer_norm(v, g, b):
    mu = jnp.mean(v, axis=-1, keepdims=True)
    d = v - mu
    var = jnp.mean(d * d, axis=-1, keepdims=True)
    return d * lax.rsqrt(var + LN_EPS) * g + b


def _ada_kernel(cond_ref, w_ref, b_ref, o_ref):
    cnd = cond_ref[...]
    act = (cnd * _sigmoid(cnd)).astype(BF16)
    o_ref[0] = jnp.dot(act, w_ref[0].astype(BF16), preferred_element_type=F32) + b_ref[0]


def _ada_call(cond, w_ada, b_ada):
    depth, d, n = w_ada.shape
    rows = cond.shape[0]
    tn = 1024
    return pl.pallas_call(
        _ada_kernel,
        grid=(depth, n // tn),
        in_specs=[
            pl.BlockSpec((rows, d), lambda l, j: (0, 0)),
            pl.BlockSpec((1, d, tn), lambda l, j: (l, 0, j)),
            pl.BlockSpec((1, 1, tn), lambda l, j: (l, 0, j)),
        ],
        out_specs=pl.BlockSpec((1, rows, tn), lambda l, j: (l, 0, j)),
        out_shape=jax.ShapeDtypeStruct((depth, rows, n), F32),
        compiler_params=_params("parallel", "parallel"),
        name="ada_mod",
    )(cond, w_ada, b_ada.reshape(depth, 1, n))


def _proj_kernel(*refs, rope):
    if rope:
        x_ref, mod_ref, w_ref, b_ref, cos_ref, sinm_ref, sinp_ref, o_ref, u_ref = refs
    else:
        x_ref, mod_ref, w_ref, b_ref, o_ref, u_ref = refs
    n = pl.program_id(1)

    @pl.when(n == 0)
    def _():
        sh = mod_ref[0, 0:1, :]
        sc = mod_ref[0, 1:2, :]
        u_ref[...] = (x_ref[...] * (1.0 + sc) + sh).astype(BF16)

    acc = jnp.dot(u_ref[...], w_ref[...], preferred_element_type=F32) + b_ref[...]
    is_q = jnp.logical_or(n == T_DQ, n == T_NQ)
    scale = jnp.where(is_q, Q_SCALE, 1.0).astype(F32)
    if rope:
        is_rope = jnp.logical_or(n == T_DQ, n == T_DK)

        @pl.when(is_rope)
        def _():
            cs = cos_ref[...]
            sm = sinm_ref[...]
            sp = sinp_ref[...]
            for g in range(PROJ_TN // LANES):
                xg = acc[:, g * LANES:(g + 1) * LANES]
                rot = xg * cs + pltpu.roll(xg, LANES - DIFF_D // 4, 1) * sm + pltpu.roll(xg, DIFF_D // 4, 1) * sp
                o_ref[:, g * LANES:(g + 1) * LANES] = (rot * scale).astype(BF16)

        @pl.when(jnp.logical_not(is_rope))
        def _():
            o_ref[...] = (acc * scale).astype(BF16)
    else:
        o_ref[...] = (acc * scale).astype(BF16)


def _proj_call(x, mod, w, b, rope_tabs, seq, tm):
    m, d = x.shape
    ntile = PROJ_W // PROJ_TN
    tpm = seq // tm
    rope = rope_tabs is not None
    in_specs = [
        pl.BlockSpec((tm, d), lambda i, n: (i, 0)),
        pl.BlockSpec((1, 6, d), lambda i, n: (i // tpm, 0, 0)),
        pl.BlockSpec((d, PROJ_TN), lambda i, n: (0, n)),
        pl.BlockSpec((1, PROJ_TN), lambda i, n: (0, n)),
    ]
    args = [x, mod, w, b]
    if rope:
        in_specs += [pl.BlockSpec((tm, LANES), lambda i, n: (i % tpm, 0))] * 3
        args += list(rope_tabs)
    return pl.pallas_call(
        functools.partial(_proj_kernel, rope=rope),
        grid=(m // tm, ntile),
        in_specs=in_specs,
        out_specs=pl.BlockSpec((tm, PROJ_TN), lambda i, n: (i, n)),
        out_shape=jax.ShapeDtypeStruct((m, PROJ_W), BF16),
        scratch_shapes=[pltpu.VMEM((tm, d), BF16)],
        compiler_params=_params("parallel", "arbitrary"),
        name="proj_lat" if rope else "proj_ctx",
    )(*args)


def _local_kernel(ab, ac, ax, ca, cg, acp, axp, cap, cgp, acn, axn, can, cgn,
                  caw, cab, dww, dwb, lng, lnb, ya_ref, yb_ref, z_ref, glu_ref, conv_ref, *, tt, tps):
    i = pl.program_id(0)
    si = i % tps
    has_prev = (si > 0).astype(F32)
    has_next = (si < tps - 1).astype(F32)
    n_ext = tt + 2 * HALO

    def glu(a, g):
        return a[...].astype(F32) * _sigmoid(g[...].astype(F32))

    def prod(a, b):
        return a[...].astype(F32) * b[...].astype(F32)

    ngrp = BRANCH_W // LANES
    for dst, pieces in ((glu_ref, (glu(cap, cgp) * has_prev, glu(ca, cg), glu(can, cgn) * has_next)),
                        (z_ref, (prod(acp, axp) * has_prev, prod(ac, ax), prod(acn, axn) * has_next))):
        for g in range(ngrp):
            ls = slice(g * LANES, (g + 1) * LANES)
            dst[0, g, 0:HALO, :] = pieces[0][:, ls]
            dst[0, g, HALO:HALO + tt, :] = pieces[1][:, ls]
            dst[0, g, HALO + tt:n_ext, :] = pieces[2][:, ls]
    for g in range(ngrp):
        for sft in range(1, SUBLANES):
            glu_ref[sft, g, 0:n_ext - SUBLANES, :] = glu_ref[0, g, pl.ds(sft, n_ext - SUBLANES), :]
        for idx, sft in enumerate(Z_SHIFTS[1:], start=1):
            z_ref[idx, g, 0:n_ext - SUBLANES, :] = z_ref[0, g, pl.ds(sft, n_ext - SUBLANES), :]

    rc = min(tt, 128)
    for g in range(ngrp):
        ls = slice(g * LANES, (g + 1) * LANES)

        def chunk(r, carry):
            r0 = pl.multiple_of(r * rc, rc)
            acc = None
            for k in range(CONV_A_K):
                o = HALO - CONV_A_K // 2 + k
                tap = z_ref[Z_SHIFTS.index(o % SUBLANES), g, pl.ds(r0 + (o - o % SUBLANES), rc), :] * caw[k:k + 1, ls]
                acc = tap if acc is None else acc + tap
            ya = ab[pl.ds(r0, rc), ls].astype(F32) * (acc + cab[:, ls])
            ya_ref[pl.ds(r0, rc), ls] = ya.astype(BF16)

            acc = None
            for k in range(CONF_K):
                o = HALO - CONF_K // 2 + k
                tap = glu_ref[o % SUBLANES, g, pl.ds(r0 + (o - o % SUBLANES), rc), :] * dww[k:k + 1, ls]
                acc = tap if acc is None else acc + tap
            conv_ref[pl.ds(r0, rc), ls] = acc + dwb[:, ls]
            return carry

        lax.fori_loop(0, tt // rc, chunk, 0)

    yn = _layer_norm(conv_ref[...], lng[...], lnb[...])
    yb_ref[...] = (yn * _sigmoid(yn)).astype(BF16)


def _local_call(p, caw, cab, dww, dwb, lng, lnb, seq, tt):
    m = p.shape[0]
    tps = seq // tt
    hb = tt // HALO
    last = m // HALO - 1

    def cur(t):
        return pl.BlockSpec((tt, BRANCH_W), lambda i: (i, t))

    def prev(t):
        return pl.BlockSpec((HALO, BRANCH_W), lambda i: (jnp.maximum(i * hb - 1, 0), t))

    def nxt(t):
        return pl.BlockSpec((HALO, BRANCH_W), lambda i: (jnp.minimum((i + 1) * hb, last), t))

    def full(a):
        return pl.BlockSpec(a.shape, lambda i: (0,) * a.ndim)

    small = [caw, cab.reshape(1, -1), dww, dwb.reshape(1, -1), lng.reshape(1, -1), lnb.reshape(1, -1)]
    in_specs = ([cur(t) for t in (T_AB, T_AC, T_AX, T_CA, T_CG)]
                + [prev(t) for t in (T_AC, T_AX, T_CA, T_CG)]
                + [nxt(t) for t in (T_AC, T_AX, T_CA, T_CG)]
                + [full(a) for a in small])
    out = jax.ShapeDtypeStruct((m, BRANCH_W), BF16)
    return pl.pallas_call(
        functools.partial(_local_kernel, tt=tt, tps=tps),
        grid=(m // tt,),
        in_specs=in_specs,
        out_specs=[pl.BlockSpec((tt, BRANCH_W), lambda i: (i, 0))] * 2,
        out_shape=[out, out],
        scratch_shapes=[pltpu.VMEM((len(Z_SHIFTS), BRANCH_W // LANES, tt + 2 * HALO + SUBLANES, LANES), F32),
                        pltpu.VMEM((SUBLANES, BRANCH_W // LANES, tt + 2 * HALO + SUBLANES, LANES), F32),
                        pltpu.VMEM((tt, BRANCH_W), F32)],
        compiler_params=_params("parallel"),
        name="local_branches",
    )(*([p] * 13), *small)


def _diff_kernel(*refs, tq, tk, n_lat, lam_init, unroll):
    if n_lat:
        dl_ref, q_ref, kc_ref, vc_ref, k_ref, v_ref, g_ref, o_ref, m_ref, acc_ref, vca_ref, va_ref = refs
    else:
        dl_ref, q_ref, kc_ref, vc_ref, g_ref, o_ref, m_ref, acc_ref, vca_ref = refs
    dl = dl_ref[...]
    lam = (jnp.exp(jnp.sum(dl[0:1] * dl[1:2], axis=-1, keepdims=True))
           - jnp.exp(jnp.sum(dl[2:3] * dl[3:4], axis=-1, keepdims=True)) + lam_init)

    @pl.when(pl.program_id(2) == 0)
    def _():
        vca_ref[:, :DIFF_V] = vc_ref[...]
        vca_ref[:, DIFF_V:] = jnp.ones((vca_ref.shape[0], DIFF_V), BF16)
        if n_lat:
            va_ref[:, :DIFF_V] = v_ref[...]
            va_ref[:, DIFF_V:] = jnp.ones((va_ref.shape[0], DIFF_V), BF16)

    qf = q_ref[...].astype(F32)
    lane = lax.broadcasted_iota(jnp.int32, qf.shape, 1)
    q2 = jnp.concatenate([jnp.where(lane < DIFF_D, qf, 0.0), jnp.where(lane >= DIFF_D, qf, 0.0)],
                         axis=0).astype(BF16)

    m_ref[...] = jnp.full(m_ref.shape, NEG_INF, F32)
    acc_ref[...] = jnp.zeros(acc_ref.shape, F32)

    def step(kb, vb):
        nc = kb.shape[0] // LANES
        s = lax.dot_general(q2, kb, (((1,), (1,)), ((), ())), preferred_element_type=F32)
        cols = [s[:, c * LANES:(c + 1) * LANES] for c in range(nc)]
        cmax = cols[0]
        for sc in cols[1:]:
            cmax = jnp.maximum(cmax, sc)
        m_old = m_ref[...]
        m_new = jnp.maximum(m_old, jnp.max(cmax, axis=1, keepdims=True))
        alpha = jnp.exp2(m_old - m_new)
        p = jnp.concatenate([jnp.exp2(sc - m_new) for sc in cols], axis=1).astype(BF16)
        pv = jnp.dot(p, vb, preferred_element_type=F32)
        acc_ref[...] = jnp.concatenate([alpha, alpha], axis=1) * acc_ref[...] + pv
        m_ref[...] = m_new

    step(kc_ref[...], vca_ref[...])
    if n_lat:
        def body(j, carry):
            j0 = pl.multiple_of(j * tk, tk)
            step(k_ref[pl.ds(j0, tk), :], va_ref[pl.ds(j0, tk), :])
            return carry
        lax.fori_loop(0, n_lat // tk, body, 0, unroll=unroll)

    acc = acc_ref[...]
    o = acc[:, :DIFF_V] / acc[:, DIFF_V:]
    od = o[:tq] - lam * o[tq:]
    ms = jnp.mean(od * od, axis=-1, keepdims=True)
    o_ref[...] = (od * lax.rsqrt(ms + LN_EPS) * g_ref[...] * (1.0 - lam_init)).astype(BF16)


def _diff_call(dl, pq, pc, pk, g, lam_init, batch, q_seq, ctx_len, lat_seq, tq, tk):
    nq = q_seq // tq
    in_specs = [
        pl.BlockSpec(dl.shape, lambda b, h, i: (0, 0)),
        pl.BlockSpec((tq, LANES), lambda b, h, i: (b * nq + i, C_DQ + h)),
        pl.BlockSpec((ctx_len, LANES), lambda b, h, i: (b, C_DK + h)),
        pl.BlockSpec((ctx_len, LANES), lambda b, h, i: (b, C_DV + h)),
    ]
    args = [dl, pq, pc, pc]
    if pk is not None:
        in_specs += [pl.BlockSpec((lat_seq, LANES), lambda b, h, i: (b, C_DK + h)),
                     pl.BlockSpec((lat_seq, LANES), lambda b, h, i: (b, C_DV + h))]
        args += [pk, pk]
    in_specs.append(pl.BlockSpec((1, DIFF_V), lambda b, h, i: (0, 0)))
    args.append(g.reshape(1, DIFF_V))
    return pl.pallas_call(
        functools.partial(_diff_kernel, tq=tq, tk=tk, n_lat=lat_seq if pk is not None else 0, lam_init=lam_init,
                          unroll=8),
        grid=(batch, DIFF_HEADS, nq),
        in_specs=in_specs,
        out_specs=pl.BlockSpec((tq, DIFF_V), lambda b, h, i: (b * nq + i, h)),
        out_shape=jax.ShapeDtypeStruct((batch * q_seq, BRANCH_W), BF16),
        scratch_shapes=([pltpu.VMEM((2 * tq, LANES), F32), pltpu.VMEM((2 * tq, 2 * DIFF_V), F32),
                         pltpu.VMEM((ctx_len, 2 * DIFF_V), BF16)]
                        + ([pltpu.VMEM((lat_seq, 2 * DIFF_V), BF16)] if pk is not None else [])),
        compiler_params=_params("arbitrary", "arbitrary", "arbitrary"),
        name="diff_attn_lat" if pk is not None else "diff_attn_ctx",
    )(*args)


def _na_kernel(*refs, rows, lat):
    if lat:
        q_ref, kc_ref, vc_ref, k_ref, v_ref, bias_ref, o_ref = refs
    else:
        q_ref, kc_ref, vc_ref, o_ref = refs
    if lat:
        i = pl.program_id(2)
        kw0 = jnp.clip(i * NA_QROWS - NA_WIN_R // 2, 0, rows - NA_KROWS)
        start = pl.multiple_of(kw0 * GRID_W, NA_WIN_R // 2 * GRID_W)
    nt = (((1,), (1,)), ((), ()))
    for pp in range(q_ref.shape[1] // LANES):
        ls = slice(pp * LANES, (pp + 1) * LANES)
        qf = q_ref[:, ls].astype(F32)
        lane = lax.broadcasted_iota(jnp.int32, qf.shape, 1)
        kc = kc_ref[:, ls]
        vc = vc_ref[:, ls]
        if lat:
            kw = k_ref[pl.ds(start, NA_KROWS * GRID_W), ls]
            vw = v_ref[pl.ds(start, NA_KROWS * GRID_W), ls]
        outs = []
        for hh in range(2):
            sel = (lane < NA_D) if hh == 0 else (lane >= NA_D)
            qz = jnp.where(sel, qf, 0.0).astype(BF16)
            s_c = lax.dot_general(qz, kc, nt, preferred_element_type=F32)
            m = jnp.max(s_c, axis=-1, keepdims=True)
            if lat:
                s_l = (lax.dot_general(qz, kw, nt, preferred_element_type=F32)
                       + bias_ref[0, 2 * pp + hh].astype(F32))
                m = jnp.maximum(m, jnp.max(s_l, axis=-1, keepdims=True))
            p_c = jnp.exp2(s_c - m)
            l = jnp.sum(p_c, axis=-1, keepdims=True)
            o = jnp.dot(p_c.astype(BF16), vc, preferred_element_type=F32)
            if lat:
                p_l = jnp.exp2(s_l - m)
                l = l + jnp.sum(p_l, axis=-1, keepdims=True)
                o = o + jnp.dot(p_l.astype(BF16), vw, preferred_element_type=F32)
            outs.append(o / l)
        o_ref[:, ls] = jnp.where(lane < NA_D, outs[0], outs[1]).astype(BF16)


def _na_bias_table(rpb, rows):
    win_r = NA_WIN_R
    hi = lax.Precision.HIGHEST
    cidx = jnp.arange(GRID_W)
    cstart = jnp.clip(cidx - NA_WIN_C // 2, 0, GRID_W - NA_WIN_C)
    col_ok = (cidx[None, :] >= cstart[:, None]) & (cidx[None, :] < cstart[:, None] + NA_WIN_C)
    dc = cidx[None, :] - cidx[:, None] + NA_WIN_C - 1
    oh_c = (dc[:, :, None] == jnp.arange(2 * NA_WIN_C - 1)).astype(F32)
    oh_r, row_ok = [], []
    for r_b in (0, NA_QROWS, rows - NA_QROWS):
        kw0 = min(max(r_b - win_r // 2, 0), rows - NA_KROWS)
        qr = r_b + jnp.arange(NA_QROWS)
        r0 = jnp.clip(qr - win_r // 2, 0, rows - win_r)
        kr = kw0 + jnp.arange(NA_KROWS)
        row_ok.append((kr[None, :] >= r0[:, None]) & (kr[None, :] < r0[:, None] + win_r))
        dr = kr[None, :] - qr[:, None] + win_r - 1
        oh_r.append((dr[:, :, None] == jnp.arange(2 * win_r - 1)).astype(F32))
    oh_r, row_ok = jnp.stack(oh_r), jnp.stack(row_ok)
    tc = jnp.einsum('lhrc,xyc->lhrxy', rpb.astype(F32), oh_c, precision=hi)
    t = jnp.einsum('pqkr,lhrxy->lphqxky', oh_r, tc, precision=hi)
    ok = row_ok[:, :, None, :, None] & col_ok[None, None, :, None, :]
    t = jnp.where(ok[None, :, None], t * math.log2(math.e), NEG_INF)
    return t.astype(BF16).reshape(rpb.shape[0], 3, rpb.shape[1], NA_QROWS * GRID_W, NA_KROWS * GRID_W)


def _na_lat_call(p, pc, bias, batch, seq, ctx_len):
    rows = seq // GRID_W
    tq = NA_QROWS * GRID_W
    nblk = seq // tq
    w = NA_HPS * NA_D
    cq, ck, cv = (c * LANES // w for c in (C_NQ, C_NK, C_NV))

    def pat(i):
        return jnp.where(i == 0, 0, jnp.where(i == nblk - 1, 2, 1))

    return pl.pallas_call(
        functools.partial(_na_kernel, rows=rows, lat=True),
        grid=(batch, NA_HEADS // NA_HPS, nblk),
        in_specs=[
            pl.BlockSpec((tq, w), lambda b, h, i: (b * nblk + i, cq + h)),
            pl.BlockSpec((ctx_len, w), lambda b, h, i: (b, ck + h)),
            pl.BlockSpec((ctx_len, w), lambda b, h, i: (b, cv + h)),
            pl.BlockSpec((seq, w), lambda b, h, i: (b, ck + h)),
            pl.BlockSpec((seq, w), lambda b, h, i: (b, cv + h)),
            pl.BlockSpec((1, NA_HPS, tq, NA_KROWS * GRID_W), lambda b, h, i: (pat(i), h, 0, 0)),
        ],
        out_specs=pl.BlockSpec((tq, w), lambda b, h, i: (b * nblk + i, h)),
        out_shape=jax.ShapeDtypeStruct((batch * seq, BRANCH_W), BF16),
        compiler_params=_params("parallel", "parallel", "arbitrary"),
        name="na_attn_lat",
    )(p, pc, pc, p, p, bias)


def _na_ctx_call(pc, batch, ctx_len):
    return pl.pallas_call(
        functools.partial(_na_kernel, rows=0, lat=False),
        grid=(batch, NA_HEADS // 2),
        in_specs=[
            pl.BlockSpec((ctx_len, LANES), lambda b, h: (b, C_NQ + h)),
            pl.BlockSpec((ctx_len, LANES), lambda b, h: (b, C_NK + h)),
            pl.BlockSpec((ctx_len, LANES), lambda b, h: (b, C_NV + h)),
        ],
        out_specs=pl.BlockSpec((ctx_len, LANES), lambda b, h: (b, h)),
        out_shape=jax.ShapeDtypeStruct((batch * ctx_len, BRANCH_W), BF16),
        compiler_params=_params("parallel", "parallel"),
        name="na_attn_ctx",
    )(pc, pc, pc)


def _merge_kernel(x_ref, mod_ref, g0, g1, g2, g3, y0, y1, y2, y3, wb_ref, bb_ref, wo_ref, lng, lnb, o_ref, *, alpha):
    merged = None
    for idx, (g_ref, y_ref) in enumerate(((g0, y0), (g1, y1), (g2, y2), (g3, y3))):
        br = jnp.dot(y_ref[...], wb_ref[idx], preferred_element_type=F32) + bb_ref[idx:idx + 1, :]
        term = _sigmoid(g_ref[...].astype(F32)) * br
        merged = term if merged is None else merged + term
    y = jnp.dot(merged.astype(BF16), wo_ref[...], preferred_element_type=F32)
    gate = mod_ref[0, 2:3, :]
    o_ref[...] = _layer_norm(alpha * x_ref[...] + gate * y, lng[...], lnb[...])


def _merge_call(x, mod, p, ys, wb, bb, wo, lng, lnb, seq, tm, alpha):
    m, d = x.shape
    tpm = seq // tm
    in_specs = ([pl.BlockSpec((tm, d), lambda i: (i, 0)),
                 pl.BlockSpec((1, 6, d), lambda i: (i // tpm, 0, 0))]
                + [pl.BlockSpec((tm, d), functools.partial(lambda i, k: (i, k), k=k)) for k in range(N_BRANCH)]
                + [pl.BlockSpec((tm, BRANCH_W), lambda i: (i, 0))] * N_BRANCH
                + [pl.BlockSpec(wb.shape, lambda i: (0, 0, 0)),
                   pl.BlockSpec(bb.shape, lambda i: (0, 0)),
                   pl.BlockSpec(wo.shape, lambda i: (0, 0)),
                   pl.BlockSpec((1, d), lambda i: (0, 0)),
                   pl.BlockSpec((1, d), lambda i: (0, 0))])
    return pl.pallas_call(
        functools.partial(_merge_kernel, alpha=alpha),
        grid=(m // tm,),
        in_specs=in_specs,
        out_specs=pl.BlockSpec((tm, d), lambda i: (i, 0)),
        out_shape=jax.ShapeDtypeStruct((m, d), F32),
        compiler_params=_params("parallel"),
        name="merge_out",
    )(x, mod, p, p, p, p, *ys, wb, bb, wo, lng.reshape(1, d), lnb.reshape(1, d))


def _ffn_kernel(x_ref, mod_ref, w1_ref, w2_ref, lng, lnb, o_ref, h_ref, acc_ref, *, alpha):
    k = pl.program_id(1)

    @pl.when(k == 0)
    def _():
        sh = mod_ref[0, 3:4, :]
        sc = mod_ref[0, 4:5, :]
        h_ref[...] = (x_ref[...] * (1.0 + sc) + sh).astype(BF16)
        acc_ref[...] = jnp.zeros(acc_ref.shape, F32)

    a = jnp.maximum(jnp.dot(h_ref[...], w1_ref[...], preferred_element_type=F32), 0.0)
    acc_ref[...] += jnp.dot((a * a).astype(BF16), w2_ref[...], preferred_element_type=F32)

    @pl.when(k == pl.num_programs(1) - 1)
    def _():
        gate = mod_ref[0, 5:6, :]
        o_ref[...] = _layer_norm(alpha * x_ref[...] + gate * acc_ref[...], lng[...], lnb[...])


def _ffn_call(x, mod, w1, w2, lng, lnb, seq, tm, kf, alpha):
    m, d = x.shape
    tpm = seq // tm
    return pl.pallas_call(
        functools.partial(_ffn_kernel, alpha=alpha),
        grid=(m // tm, D_FF // kf),
        in_specs=[
            pl.BlockSpec((tm, d), lambda i, k: (i, 0)),
            pl.BlockSpec((1, 6, d), lambda i, k: (i // tpm, 0, 0)),
            pl.BlockSpec((d, kf), lambda i, k: (0, k)),
            pl.BlockSpec((kf, d), lambda i, k: (k, 0)),
            pl.BlockSpec((1, d), lambda i, k: (0, 0)),
            pl.BlockSpec((1, d), lambda i, k: (0, 0)),
        ],
        out_specs=pl.BlockSpec((tm, d), lambda i, k: (i, 0)),
        out_shape=jax.ShapeDtypeStruct((m, d), F32),
        scratch_shapes=[pltpu.VMEM((tm, d), BF16), pltpu.VMEM((tm, d), F32)],
        compiler_params=_params("parallel", "arbitrary"),
        name="ffn",
    )(x, mod, w1, w2, lng.reshape(1, d), lnb.reshape(1, d))


def _rope_tables(seq):
    nf = DIFF_D // 4
    t = jnp.arange(seq)
    freqs = jnp.power(ROPE_BASE, -jnp.arange(nf, dtype=F32) / nf)
    pos = jnp.stack([t // GRID_W, t % GRID_W], axis=-1).astype(F32)
    ang = pos[:, :, None] * freqs
    cos, sin = jnp.cos(ang), jnp.sin(ang)
    shape = (seq, LANES // DIFF_D, 2, 2, nf)
    cos_t = jnp.broadcast_to(cos[:, None, :, None, :], shape)
    sin_t = jnp.broadcast_to(sin[:, None, :, None, :], shape)
    half = jnp.arange(2)[None, None, None, :, None]
    sin_m = jnp.where(half == 0, -sin_t, 0.0)
    sin_p = jnp.where(half == 1, sin_t, 0.0)
    return tuple(a.reshape(seq, LANES) for a in (cos_t, sin_m, sin_p))


def kernel(x, c, ctx, c_ctx, w_ada, b_ada, w_in, b_in, conv_a_w, conv_a_b, conf_dw_w, conf_dw_b, conf_ln_g, conf_ln_b, diff_lambda, diff_norm_g, na_rpb, w_branch, b_branch, w_o, ln_g, ln_b, w_ff1, w_ff2):
    batch, seq, d = x.shape
    ctx_len = ctx.shape[1]
    depth = w_ada.shape[0]
    rows = seq // GRID_W
    assert d == D_MODEL and seq % (NA_QROWS * GRID_W) == 0 and rows >= NA_KROWS
    alpha = (2 * depth) ** 0.25

    cond_rows = -(-(batch + 1) // 8) * 8
    cond = jnp.zeros((cond_rows, d), F32).at[:batch].set(c).at[batch].set(c_ctx)
    mods = _ada_call(cond, w_ada, b_ada)

    w_in_p = jnp.concatenate([w_in[:, :, MAIN_W:], w_in[:, :, :MAIN_W]], axis=-1).astype(BF16)
    b_in_p = jnp.concatenate([b_in[:, MAIN_W:], b_in[:, :MAIN_W]], axis=-1).reshape(depth, 1, PROJ_W)
    w_branch_b = w_branch.astype(BF16)
    w_o_b = w_o.astype(BF16)
    w_ff1_b = w_ff1.astype(BF16)
    w_ff2_b = w_ff2.astype(BF16)
    rope_tabs = _rope_tables(seq)

    tm_lat = min(2048, seq)
    tq = min(256, seq)
    tk = min(1024, seq)
    na_bias = _na_bias_table(na_rpb, rows)
    m_ctx = batch * ctx_len

    xl = x.reshape(batch * seq, d)
    xc = ctx.reshape(m_ctx, d)
    for l in range(depth):
        last = l == depth - 1
        lam_init = 0.8 - 0.6 * math.exp(-0.3 * l)
        mod_lat = mods[l, :batch].reshape(batch, 6, d)
        mod_ctx = mods[l, batch:batch + 1].reshape(1, 6, d)
        local_w = (conv_a_w[l], conv_a_b[l], conf_dw_w[l], conf_dw_b[l], conf_ln_g[l], conf_ln_b[l])

        pc = _proj_call(xc, mod_ctx, w_in_p[l], b_in_p[l], None, m_ctx, m_ctx)
        p = _proj_call(xl, mod_lat, w_in_p[l], b_in_p[l], rope_tabs, seq, tm_lat)

        y_a, y_b = _local_call(p, *local_w, seq, min(512, seq))
        y_c = _diff_call(diff_lambda[l], p, pc, p, diff_norm_g[l], lam_init, batch, seq, ctx_len, seq, tq, tk)
        y_d = _na_lat_call(p, pc, na_bias[l], batch, seq, ctx_len)
        x1 = _merge_call(xl, mod_lat, p, (y_a, y_b, y_c, y_d), w_branch_b[l], b_branch[l], w_o_b[l],
                         ln_g[l, 0], ln_b[l, 0], seq, min(512, seq), alpha)
        xl = _ffn_call(x1, mod_lat, w_ff1_b[l], w_ff2_b[l], ln_g[l, 1], ln_b[l, 1], seq, min(1024, seq), 512, alpha)

        if not last:
            yc_a, yc_b = _local_call(pc, *local_w, ctx_len, ctx_len)
            yc_c = _diff_call(diff_lambda[l], pc, pc, None, diff_norm_g[l], lam_init, batch, ctx_len, ctx_len, 0,
                              ctx_len, tk)
            yc_d = _na_ctx_call(pc, batch, ctx_len)
            xc1 = _merge_call(xc, mod_ctx, pc, (yc_a, yc_b, yc_c, yc_d), w_branch_b[l], b_branch[l], w_o_b[l],
                              ln_g[l, 0], ln_b[l, 0], m_ctx, min(512, m_ctx), alpha)
            xc = _ffn_call(xc1, mod_ctx, w_ff1_b[l], w_ff2_b[l], ln_g[l, 1], ln_b[l, 1], m_ctx, min(1024, m_ctx),
                           512, alpha)
    return xl.reshape(batch, seq, d)
```

```python
import functools
import math

import jax
import jax.numpy as jnp
from jax import lax
from jax.experimental import pallas as pl
from jax.experimental.pallas import tpu as pltpu

F32 = jnp.float32
BF16 = jnp.bfloat16

D_MODEL = 1024
GRID_W = 64
BRANCH_W = D_MODEL // 2
N_BRANCH = 4
CONV_A_K = 3
CONF_K = 31
DIFF_HEADS = 4
DIFF_D = D_MODEL // 16
DIFF_V = 2 * DIFF_D
NA_HEADS = 8
NA_D = D_MODEL // 16
NA_WIN_R = 8
NA_WIN_C = 16
D_FF = 4 * D_MODEL
ROPE_BASE = 10000.0
LN_EPS = 1e-5
NEG_INF = -1e30

GATE_W = N_BRANCH * D_MODEL
MAIN_W = 11 * BRANCH_W
PROJ_W = GATE_W + MAIN_W
PROJ_TN = 512
T_GATE_END = GATE_W // PROJ_TN
T_AB, T_AC, T_AX, T_CA, T_CG, T_DQ, T_DK, T_DV, T_NQ, T_NK, T_NV = range(T_GATE_END, T_GATE_END + 11)
LANES = 128
C_DQ, C_DK, C_DV, C_NQ, C_NK, C_NV = (t * (PROJ_TN // LANES) for t in (T_DQ, T_DK, T_DV, T_NQ, T_NK, T_NV))

HALO = 16
SUBLANES = 8
Z_SHIFTS = tuple(sorted({(HALO - CONV_A_K // 2 + k) % SUBLANES for k in range(CONV_A_K)}))
NA_QROWS = 8
NA_KROWS = 16
NA_HPS = 4
VMEM_LIMIT = 56 * 1024 * 1024
Q_SCALE = DIFF_D ** -0.5 * math.log2(math.e)


def _params(*sem):
    return pltpu.CompilerParams(dimension_semantics=sem, vmem_limit_bytes=VMEM_LIMIT)


def _sigmoid(v):
    return 1.0 / (1.0 + jnp.exp(-v))


def _layer_norm(v, g, b):
    mu = jnp.mean(v, axis=-1, keepdims=True)
    d = v - mu
    var = jnp.mean(d * d, axis=-1, keepdims=True)
    return d * lax.rsqrt(var + LN_EPS) * g + b


def _ada_kernel(cond_ref, w_ref, b_ref, o_ref):
    cnd = cond_ref[...]
    act = (cnd * _sigmoid(cnd)).astype(BF16)
    o_ref[0] = jnp.dot(act, w_ref[0].astype(BF16), preferred_element_type=F32) + b_ref[0]


def _ada_call(cond, w_ada, b_ada):
    depth, d, n = w_ada.shape
    rows = cond.shape[0]
    tn = 1024
    return pl.pallas_call(
        _ada_kernel,
        grid=(depth, n // tn),
        in_specs=[
            pl.BlockSpec((rows, d), lambda l, j: (0, 0)),
            pl.BlockSpec((1, d, tn), lambda l, j: (l, 0, j)),
            pl.BlockSpec((1, 1, tn), lambda l, j: (l, 0, j)),
        ],
        out_specs=pl.BlockSpec((1, rows, tn), lambda l, j: (l, 0, j)),
        out_shape=jax.ShapeDtypeStruct((depth, rows, n), F32),
        compiler_params=_params("parallel", "parallel"),
        name="ada_mod",
    )(cond, w_ada, b_ada.reshape(depth, 1, n))


def _proj_kernel(*refs, rope):
    if rope:
        x_ref, mod_ref, w_ref, b_ref, cos_ref, sinm_ref, sinp_ref, o_ref, u_ref = refs
    else:
        x_ref, mod_ref, w_ref, b_ref, o_ref, u_ref = refs
    n = pl.program_id(1)

    @pl.when(n == 0)
    def _():
        sh = mod_ref[0, 0:1, :]
        sc = mod_ref[0, 1:2, :]
        u_ref[...] = (x_ref[...] * (1.0 + sc) + sh).astype(BF16)

    is_q = jnp.logical_or(n == T_DQ, n == T_NQ)
    scale = jnp.where(is_q, Q_SCALE, 1.0).astype(F32)
    half = u_ref.shape[0] // 2
    for hf in range(2):
        rs = slice(hf * half, (hf + 1) * half)
        acc = jnp.dot(u_ref[rs, :], w_ref[...], preferred_element_type=F32) + b_ref[...]
        o_ref[rs, :] = (acc * scale).astype(BF16)
    if rope:
        @pl.when(jnp.logical_or(n == T_DQ, n == T_DK))
        def _():
            for hf in range(2):
                rs = slice(hf * half, (hf + 1) * half)
                acc = jnp.dot(u_ref[rs, :], w_ref[...], preferred_element_type=F32) + b_ref[...]
                cs = cos_ref[rs, :]
                sm = sinm_ref[rs, :]
                sp = sinp_ref[rs, :]
                for g in range(PROJ_TN // LANES):
                    xg = acc[:, g * LANES:(g + 1) * LANES]
                    rot = (xg * cs + pltpu.roll(xg, LANES - DIFF_D // 4, 1) * sm
                           + pltpu.roll(xg, DIFF_D // 4, 1) * sp)
                    o_ref[rs, g * LANES:(g + 1) * LANES] = (rot * scale).astype(BF16)


def _proj_call(x, mod, w, b, rope_tabs, seq, tm):
    m, d = x.shape
    ntile = PROJ_W // PROJ_TN
    tpm = seq // tm
    rope = rope_tabs is not None
    in_specs = [
        pl.BlockSpec((tm, d), lambda i, n: (i, 0)),
        pl.BlockSpec((1, 6, d), lambda i, n: (i // tpm, 0, 0)),
        pl.BlockSpec((d, PROJ_TN), lambda i, n: (0, n)),
        pl.BlockSpec((1, PROJ_TN), lambda i, n: (0, n)),
    ]
    args = [x, mod, w, b]
    if rope:
        in_specs += [pl.BlockSpec((tm, LANES), lambda i, n: (i % tpm, 0))] * 3
        args += list(rope_tabs)
    return pl.pallas_call(
        functools.partial(_proj_kernel, rope=rope),
        grid=(m // tm, ntile),
        in_specs=in_specs,
        out_specs=pl.BlockSpec((tm, PROJ_TN), lambda i, n: (i, n)),
        out_shape=jax.ShapeDtypeStruct((m, PROJ_W), BF16),
        scratch_shapes=[pltpu.VMEM((tm, d), BF16)],
        compiler_params=_params("parallel", "arbitrary"),
        name="proj_lat" if rope else "proj_ctx",
    )(*args)


def _local_kernel(ab, ac, ax, ca, cg, acp, axp, cap, cgp, acn, axn, can, cgn,
                  caw, cab, dww, dwb, lng, lnb, ya_ref, yb_ref, z_ref, glu_ref, conv_ref, *, tt, tps):
    i = pl.program_id(0)
    si = i % tps
    has_prev = (si > 0).astype(F32)
    has_next = (si < tps - 1).astype(F32)
    n_ext = tt + 2 * HALO

    def glu(a, g):
        return a[...].astype(F32) * _sigmoid(g[...].astype(F32))

    def prod(a, b):
        return a[...].astype(F32) * b[...].astype(F32)

    ngrp = BRANCH_W // LANES
    for dst, pieces in ((glu_ref, (glu(cap, cgp) * has_prev, glu(ca, cg), glu(can, cgn) * has_next)),
                        (z_ref, (prod(acp, axp) * has_prev, prod(ac, ax), prod(acn, axn) * has_next))):
        for g in range(ngrp):
            ls = slice(g * LANES, (g + 1) * LANES)
            dst[0, g, 0:HALO, :] = pieces[0][:, ls]
            dst[0, g, HALO:HALO + tt, :] = pieces[1][:, ls]
            dst[0, g, HALO + tt:n_ext, :] = pieces[2][:, ls]
    for g in range(ngrp):
        for sft in range(1, SUBLANES):
            glu_ref[sft, g, 0:n_ext - SUBLANES, :] = glu_ref[0, g, pl.ds(sft, n_ext - SUBLANES), :]
        for idx, sft in enumerate(Z_SHIFTS[1:], start=1):
            z_ref[idx, g, 0:n_ext - SUBLANES, :] = z_ref[0, g, pl.ds(sft, n_ext - SUBLANES), :]

    rc = min(tt, 128)
    for g in range(ngrp):
        ls = slice(g * LANES, (g + 1) * LANES)

        def chunk(r, carry):
            r0 = pl.multiple_of(r * rc, rc)
            acc = None
            for k in range(CONV_A_K):
                o = HALO - CONV_A_K // 2 + k
                tap = z_ref[Z_SHIFTS.index(o % SUBLANES), g, pl.ds(r0 + (o - o % SUBLANES), rc), :] * caw[k:k + 1, ls]
                acc = tap if acc is None else acc + tap
            ya = ab[pl.ds(r0, rc), ls].astype(F32) * (acc + cab[:, ls])
            ya_ref[pl.ds(r0, rc), ls] = ya.astype(BF16)

            acc = None
            for k in range(CONF_K):
                o = HALO - CONF_K // 2 + k
                tap = glu_ref[o % SUBLANES, g, pl.ds(r0 + (o - o % SUBLANES), rc), :] * dww[k:k + 1, ls]
                acc = tap if acc is None else acc + tap
            conv_ref[pl.ds(r0, rc), ls] = acc + dwb[:, ls]
            return carry

        lax.fori_loop(0, tt // rc, chunk, 0)

    yn = _layer_norm(conv_ref[...], lng[...], lnb[...])
    yb_ref[...] = (yn * _sigmoid(yn)).astype(BF16)


def _local_call(p, caw, cab, dww, dwb, lng, lnb, seq, tt):
    m = p.shape[0]
    tps = seq // tt
    hb = tt // HALO
    last = m // HALO - 1

    def cur(t):
        return pl.BlockSpec((tt, BRANCH_W), lambda i: (i, t))

    def prev(t):
        return pl.BlockSpec((HALO, BRANCH_W), lambda i: (jnp.maximum(i * hb - 1, 0), t))

    def nxt(t):
        return pl.BlockSpec((HALO, BRANCH_W), lambda i: (jnp.minimum((i + 1) * hb, last), t))

    def full(a):
        return pl.BlockSpec(a.shape, lambda i: (0,) * a.ndim)

    small = [caw, cab.reshape(1, -1), dww, dwb.reshape(1, -1), lng.reshape(1, -1), lnb.reshape(1, -1)]
    in_specs = ([cur(t) for t in (T_AB, T_AC, T_AX, T_CA, T_CG)]
                + [prev(t) for t in (T_AC, T_AX, T_CA, T_CG)]
                + [nxt(t) for t in (T_AC, T_AX, T_CA, T_CG)]
                + [full(a) for a in small])
    out = jax.ShapeDtypeStruct((m, BRANCH_W), BF16)
    return pl.pallas_call(
        functools.partial(_local_kernel, tt=tt, tps=tps),
        grid=(m // tt,),
        in_specs=in_specs,
        out_specs=[pl.BlockSpec((tt, BRANCH_W), lambda i: (i, 0))] * 2,
        out_shape=[out, out],
        scratch_shapes=[pltpu.VMEM((len(Z_SHIFTS), BRANCH_W // LANES, tt + 2 * HALO + SUBLANES, LANES), F32),
                        pltpu.VMEM((SUBLANES, BRANCH_W // LANES, tt + 2 * HALO + SUBLANES, LANES), F32),
                        pltpu.VMEM((tt, BRANCH_W), F32)],
        compiler_params=_params("parallel"),
        name="local_branches",
    )(*([p] * 13), *small)


def _diff_kernel(*refs, tq, tk, n_lat, lam_init, unroll):
    if n_lat:
        dl_ref, q_ref, kc_ref, vc_ref, k_ref, v_ref, g_ref, o_ref, m_ref, acc_ref, vca_ref, va_ref = refs
    else:
        dl_ref, q_ref, kc_ref, vc_ref, g_ref, o_ref, m_ref, acc_ref, vca_ref = refs
    dl = dl_ref[...]
    lam = (jnp.exp(jnp.sum(dl[0:1] * dl[1:2], axis=-1, keepdims=True))
           - jnp.exp(jnp.sum(dl[2:3] * dl[3:4], axis=-1, keepdims=True)) + lam_init)

    @pl.when(pl.program_id(2) == 0)
    def _():
        vca_ref[:, :DIFF_V] = vc_ref[...]
        vca_ref[:, DIFF_V:] = jnp.ones((vca_ref.shape[0], DIFF_V), BF16)
        if n_lat:
            va_ref[:, :DIFF_V] = v_ref[...]
            va_ref[:, DIFF_V:] = jnp.ones((va_ref.shape[0], DIFF_V), BF16)

    qf = q_ref[...].astype(F32)
    lane = lax.broadcasted_iota(jnp.int32, qf.shape, 1)
    q2 = jnp.concatenate([jnp.where(lane < DIFF_D, qf, 0.0), jnp.where(lane >= DIFF_D, qf, 0.0)],
                         axis=0).astype(BF16)

    m_ref[...] = jnp.full(m_ref.shape, NEG_INF, F32)
    acc_ref[...] = jnp.zeros(acc_ref.shape, F32)

    def step(kb, vb):
        nc = kb.shape[0] // LANES
        s = lax.dot_general(q2, kb, (((1,), (1,)), ((), ())), preferred_element_type=F32)
        cols = [s[:, c * LANES:(c + 1) * LANES] for c in range(nc)]
        cmax = cols[0]
        for sc in cols[1:]:
            cmax = jnp.maximum(cmax, sc)
        m_old = m_ref[...]
        m_new = jnp.maximum(m_old, jnp.max(cmax, axis=1, keepdims=True))
        alpha = jnp.exp2(m_old - m_new)
        p = jnp.concatenate([jnp.exp2(sc - m_new) for sc in cols], axis=1).astype(BF16)
        pv = jnp.dot(p, vb, preferred_element_type=F32)
        acc_ref[...] = jnp.concatenate([alpha, alpha], axis=1) * acc_ref[...] + pv
        m_ref[...] = m_new

    step(kc_ref[...], vca_ref[...])
    if n_lat:
        def body(j, carry):
            j0 = pl.multiple_of(j * tk, tk)
            step(k_ref[pl.ds(j0, tk), :], va_ref[pl.ds(j0, tk), :])
            return carry
        lax.fori_loop(0, n_lat // tk, body, 0, unroll=unroll)

    acc = acc_ref[...]
    o = acc[:, :DIFF_V] / acc[:, DIFF_V:]
    od = o[:tq] - lam * o[tq:]
    ms = jnp.mean(od * od, axis=-1, keepdims=True)
    o_ref[...] = (od * lax.rsqrt(ms + LN_EPS) * g_ref[...] * (1.0 - lam_init)).astype(BF16)


def _diff_call(dl, pq, pc, pk, g, lam_init, batch, q_seq, ctx_len, lat_seq, tq, tk):
    nq = q_seq // tq
    in_specs = [
        pl.BlockSpec(dl.shape, lambda b, h, i: (0, 0)),
        pl.BlockSpec((tq, LANES), lambda b, h, i: (b * nq + i, C_DQ + h)),
        pl.BlockSpec((ctx_len, LANES), lambda b, h, i: (b, C_DK + h)),
        pl.BlockSpec((ctx_len, LANES), lambda b, h, i: (b, C_DV + h)),
    ]
    args = [dl, pq, pc, pc]
    if pk is not None:
        in_specs += [pl.BlockSpec((lat_seq, LANES), lambda b, h, i: (b, C_DK + h)),
                     pl.BlockSpec((lat_seq, LANES), lambda b, h, i: (b, C_DV + h))]
        args += [pk, pk]
    in_specs.append(pl.BlockSpec((1, DIFF_V), lambda b, h, i: (0, 0)))
    args.append(g.reshape(1, DIFF_V))
    return pl.pallas_call(
        functools.partial(_diff_kernel, tq=tq, tk=tk, n_lat=lat_seq if pk is not None else 0, lam_init=lam_init,
                          unroll=8),
        grid=(batch, DIFF_HEADS, nq),
        in_specs=in_specs,
        out_specs=pl.BlockSpec((tq, DIFF_V), lambda b, h, i: (b * nq + i, h)),
        out_shape=jax.ShapeDtypeStruct((batch * q_seq, BRANCH_W), BF16),
        scratch_shapes=([pltpu.VMEM((2 * tq, LANES), F32), pltpu.VMEM((2 * tq, 2 * DIFF_V), F32),
                         pltpu.VMEM((ctx_len, 2 * DIFF_V), BF16)]
                        + ([pltpu.VMEM((lat_seq, 2 * DIFF_V), BF16)] if pk is not None else [])),
        compiler_params=_params("arbitrary", "arbitrary", "arbitrary"),
        name="diff_attn_lat" if pk is not None else "diff_attn_ctx",
    )(*args)


def _na_kernel(*refs, rows, lat):
    if lat:
        q_ref, kc_ref, vc_ref, k_ref, v_ref, bias_ref, o_ref = refs
    else:
        q_ref, kc_ref, vc_ref, o_ref = refs
    if lat:
        i = pl.program_id(2)
        kw0 = jnp.clip(i * NA_QROWS - NA_WIN_R // 2, 0, rows - NA_KROWS)
        start = pl.multiple_of(kw0 * GRID_W, NA_WIN_R // 2 * GRID_W)
    nt = (((1,), (1,)), ((), ()))
    for pp in range(q_ref.shape[1] // LANES):
        ls = slice(pp * LANES, (pp + 1) * LANES)
        qf = q_ref[:, ls].astype(F32)
        lane = lax.broadcasted_iota(jnp.int32, qf.shape, 1)
        kc = kc_ref[:, ls]
        vc = vc_ref[:, ls]
        if lat:
            kw = k_ref[pl.ds(start, NA_KROWS * GRID_W), ls]
            vw = v_ref[pl.ds(start, NA_KROWS * GRID_W), ls]
        outs = []
        for hh in range(2):
            sel = (lane < NA_D) if hh == 0 else (lane >= NA_D)
            qz = jnp.where(sel, qf, 0.0).astype(BF16)
            s_c = lax.dot_general(qz, kc, nt, preferred_element_type=F32)
            m = jnp.max(s_c, axis=-1, keepdims=True)
            if lat:
                s_l = (lax.dot_general(qz, kw, nt, preferred_element_type=F32)
                       + bias_ref[0, 2 * pp + hh].astype(F32))
                m = jnp.maximum(m, jnp.max(s_l, axis=-1, keepdims=True))
            p_c = jnp.exp2(s_c - m)
            l = jnp.sum(p_c, axis=-1, keepdims=True)
            o = jnp.dot(p_c.astype(BF16), vc, preferred_element_type=F32)
            if lat:
                p_l = jnp.exp2(s_l - m)
                l = l + jnp.sum(p_l, axis=-1, keepdims=True)
                o = o + jnp.dot(p_l.astype(BF16), vw, preferred_element_type=F32)
            outs.append(o / l)
        o_ref[:, ls] = jnp.where(lane < NA_D, outs[0], outs[1]).astype(BF16)


def _na_bias_plan(rows):
    win_r = NA_WIN_R
    plan = []
    for r_b in (0, NA_QROWS, rows - NA_QROWS):
        kw0 = min(max(r_b - win_r // 2, 0), rows - NA_KROWS)
        per_q = []
        for q in range(NA_QROWS):
            qr = r_b + q
            r0 = min(max(qr - win_r // 2, 0), rows - win_r)
            per_kp = []
            for kp in range(NA_KROWS // 2):
                kl = kw0 + 2 * kp
                ok_l = r0 <= kl < r0 + win_r
                ok_r = r0 <= kl + 1 < r0 + win_r
                dr_l = kl - qr + win_r - 1
                if ok_l and ok_r:
                    per_kp.append((0, dr_l + 1))
                elif ok_l:
                    per_kp.append((1, dr_l))
                elif ok_r:
                    per_kp.append((2, dr_l + 1))
                else:
                    per_kp.append((1, 2 * win_r - 1))
            per_q.append(per_kp)
        plan.append(per_q)
    return plan


def _na_pair_blocks(rpb):
    hi = lax.Precision.HIGHEST
    cidx = jnp.arange(GRID_W)
    cstart = jnp.clip(cidx - NA_WIN_C // 2, 0, GRID_W - NA_WIN_C)
    col_ok = (cidx[None, :] >= cstart[:, None]) & (cidx[None, :] < cstart[:, None] + NA_WIN_C)
    dc = cidx[None, :] - cidx[:, None] + NA_WIN_C - 1
    oh_c = (dc[:, :, None] == jnp.arange(2 * NA_WIN_C - 1)).astype(F32)
    tc = jnp.einsum('lhrc,xyc->lhrxy', rpb.astype(F32), oh_c, precision=hi)
    tc = jnp.where(col_ok, tc * math.log2(math.e), NEG_INF)
    neg = jnp.full(tc.shape[:2] + (1,) + tc.shape[3:], NEG_INF, F32)
    blk = jnp.concatenate([tc, neg], axis=2)
    prev = jnp.concatenate([neg, tc], axis=2)
    negs = jnp.full(blk.shape, NEG_INF, F32)
    out = jnp.stack([jnp.concatenate([prev, blk], axis=-1), jnp.concatenate([blk, negs], axis=-1),
                     jnp.concatenate([negs, blk], axis=-1)], axis=2)
    return out.astype(BF16)


def _na_bias_kernel(pt_ref, o_ref, *, plan):
    for p, per_q in enumerate(plan):
        for q, per_kp in enumerate(per_q):
            for kp, (var, j) in enumerate(per_kp):
                o_ref[0, p, 0, q * GRID_W:(q + 1) * GRID_W, 2 * kp * GRID_W:2 * (kp + 1) * GRID_W] = pt_ref[0, 0, var, j]


def _na_bias_table(rpb, rows):
    depth, heads = rpb.shape[:2]
    pt = _na_pair_blocks(rpb)
    tq, tkw = NA_QROWS * GRID_W, NA_KROWS * GRID_W
    return pl.pallas_call(
        functools.partial(_na_bias_kernel, plan=_na_bias_plan(rows)),
        grid=(depth, heads),
        in_specs=[pl.BlockSpec((1, 1) + pt.shape[2:], lambda l, h: (l, h, 0, 0, 0, 0))],
        out_specs=pl.BlockSpec((1, 3, 1, tq, tkw), lambda l, h: (l, 0, h, 0, 0)),
        out_shape=jax.ShapeDtypeStruct((depth, 3, heads, tq, tkw), BF16),
        compiler_params=_params("parallel", "parallel"),
        name="na_bias_table",
    )(pt)


def _na_lat_call(p, pc, bias, batch, seq, ctx_len):
    rows = seq // GRID_W
    tq = NA_QROWS * GRID_W
    nblk = seq // tq
    w = NA_HPS * NA_D
    cq, ck, cv = (c * LANES // w for c in (C_NQ, C_NK, C_NV))

    def pat(i):
        return jnp.where(i == 0, 0, jnp.where(i == nblk - 1, 2, 1))

    return pl.pallas_call(
        functools.partial(_na_kernel, rows=rows, lat=True),
        grid=(batch, NA_HEADS // NA_HPS, nblk),
        in_specs=[
            pl.BlockSpec((tq, w), lambda b, h, i: (b * nblk + i, cq + h)),
            pl.BlockSpec((ctx_len, w), lambda b, h, i: (b, ck + h)),
            pl.BlockSpec((ctx_len, w), lambda b, h, i: (b, cv + h)),
            pl.BlockSpec((seq, w), lambda b, h, i: (b, ck + h)),
            pl.BlockSpec((seq, w), lambda b, h, i: (b, cv + h)),
            pl.BlockSpec((1, NA_HPS, tq, NA_KROWS * GRID_W), lambda b, h, i: (pat(i), h, 0, 0)),
        ],
        out_specs=pl.BlockSpec((tq, w), lambda b, h, i: (b * nblk + i, h)),
        out_shape=jax.ShapeDtypeStruct((batch * seq, BRANCH_W), BF16),
        compiler_params=_params("parallel", "parallel", "arbitrary"),
        name="na_attn_lat",
    )(p, pc, pc, p, p, bias)


def _na_ctx_call(pc, batch, ctx_len):
    return pl.pallas_call(
        functools.partial(_na_kernel, rows=0, lat=False),
        grid=(batch, NA_HEADS // 2),
        in_specs=[
            pl.BlockSpec((ctx_len, LANES), lambda b, h: (b, C_NQ + h)),
            pl.BlockSpec((ctx_len, LANES), lambda b, h: (b, C_NK + h)),
            pl.BlockSpec((ctx_len, LANES), lambda b, h: (b, C_NV + h)),
        ],
        out_specs=pl.BlockSpec((ctx_len, LANES), lambda b, h: (b, h)),
        out_shape=jax.ShapeDtypeStruct((batch * ctx_len, BRANCH_W), BF16),
        compiler_params=_params("parallel", "parallel"),
        name="na_attn_ctx",
    )(pc, pc, pc)


def _merge_kernel(x_ref, mod_ref, g0, g1, g2, g3, y0, y1, y2, y3, wb_ref, bb_ref, wo_ref, lng, lnb, o_ref, *, alpha):
    merged = None
    for idx, (g_ref, y_ref) in enumerate(((g0, y0), (g1, y1), (g2, y2), (g3, y3))):
        br = jnp.dot(y_ref[...], wb_ref[idx], preferred_element_type=F32) + bb_ref[idx:idx + 1, :]
        term = _sigmoid(g_ref[...].astype(F32)) * br
        merged = term if merged is None else merged + term
    y = jnp.dot(merged.astype(BF16), wo_ref[...], preferred_element_type=F32)
    gate = mod_ref[0, 2:3, :]
    o_ref[...] = _layer_norm(alpha * x_ref[...] + gate * y, lng[...], lnb[...])


def _merge_call(x, mod, p, ys, wb, bb, wo, lng, lnb, seq, tm, alpha):
    m, d = x.shape
    tpm = seq // tm
    in_specs = ([pl.BlockSpec((tm, d), lambda i: (i, 0)),
                 pl.BlockSpec((1, 6, d), lambda i: (i // tpm, 0, 0))]
                + [pl.BlockSpec((tm, d), functools.partial(lambda i, k: (i, k), k=k)) for k in range(N_BRANCH)]
                + [pl.BlockSpec((tm, BRANCH_W), lambda i: (i, 0))] * N_BRANCH
                + [pl.BlockSpec(wb.shape, lambda i: (0, 0, 0)),
                   pl.BlockSpec(bb.shape, lambda i: (0, 0)),
                   pl.BlockSpec(wo.shape, lambda i: (0, 0)),
                   pl.BlockSpec((1, d), lambda i: (0, 0)),
                   pl.BlockSpec((1, d), lambda i: (0, 0))])
    return pl.pallas_call(
        functools.partial(_merge_kernel, alpha=alpha),
        grid=(m // tm,),
        in_specs=in_specs,
        out_specs=pl.BlockSpec((tm, d), lambda i: (i, 0)),
        out_shape=jax.ShapeDtypeStruct((m, d), F32),
        compiler_params=_params("parallel"),
        name="merge_out",
    )(x, mod, p, p, p, p, *ys, wb, bb, wo, lng.reshape(1, d), lnb.reshape(1, d))


def _ffn_kernel(x_ref, mod_ref, w1_ref, w2_ref, lng, lnb, o_ref, h_ref, acc_ref, *, alpha):
    k = pl.program_id(1)

    @pl.when(k == 0)
    def _():
        sh = mod_ref[0, 3:4, :]
        sc = mod_ref[0, 4:5, :]
        h_ref[...] = (x_ref[...] * (1.0 + sc) + sh).astype(BF16)
        acc_ref[...] = jnp.zeros(acc_ref.shape, F32)

    a = jnp.maximum(jnp.dot(h_ref[...], w1_ref[...], preferred_element_type=F32), 0.0)
    acc_ref[...] += jnp.dot((a * a).astype(BF16), w2_ref[...], preferred_element_type=F32)

    @pl.when(k == pl.num_programs(1) - 1)
    def _():
        gate = mod_ref[0, 5:6, :]
        o_ref[...] = _layer_norm(alpha * x_ref[...] + gate * acc_ref[...], lng[...], lnb[...])


def _ffn_call(x, mod, w1, w2, lng, lnb, seq, tm, kf, alpha):
    m, d = x.shape
    tpm = seq // tm
    return pl.pallas_call(
        functools.partial(_ffn_kernel, alpha=alpha),
        grid=(m // tm, D_FF // kf),
        in_specs=[
            pl.BlockSpec((tm, d), lambda i, k: (i, 0)),
            pl.BlockSpec((1, 6, d), lambda i, k: (i // tpm, 0, 0)),
            pl.BlockSpec((d, kf), lambda i, k: (0, k)),
            pl.BlockSpec((kf, d), lambda i, k: (k, 0)),
            pl.BlockSpec((1, d), lambda i, k: (0, 0)),
            pl.BlockSpec((1, d), lambda i, k: (0, 0)),
        ],
        out_specs=pl.BlockSpec((tm, d), lambda i, k: (i, 0)),
        out_shape=jax.ShapeDtypeStruct((m, d), F32),
        scratch_shapes=[pltpu.VMEM((tm, d), BF16), pltpu.VMEM((tm, d), F32)],
        compiler_params=_params("parallel", "arbitrary"),
        name="ffn",
    )(x, mod, w1, w2, lng.reshape(1, d), lnb.reshape(1, d))


def _rope_tables(seq):
    nf = DIFF_D // 4
    t = jnp.arange(seq)
    freqs = jnp.power(ROPE_BASE, -jnp.arange(nf, dtype=F32) / nf)
    pos = jnp.stack([t // GRID_W, t % GRID_W], axis=-1).astype(F32)
    ang = pos[:, :, None] * freqs
    cos, sin = jnp.cos(ang), jnp.sin(ang)
    shape = (seq, LANES // DIFF_D, 2, 2, nf)
    cos_t = jnp.broadcast_to(cos[:, None, :, None, :], shape)
    sin_t = jnp.broadcast_to(sin[:, None, :, None, :], shape)
    half = jnp.arange(2)[None, None, None, :, None]
    sin_m = jnp.where(half == 0, -sin_t, 0.0)
    sin_p = jnp.where(half == 1, sin_t, 0.0)
    return tuple(a.reshape(seq, LANES) for a in (cos_t, sin_m, sin_p))


def kernel(x, c, ctx, c_ctx, w_ada, b_ada, w_in, b_in, conv_a_w, conv_a_b, conf_dw_w, conf_dw_b, conf_ln_g, conf_ln_b, diff_lambda, diff_norm_g, na_rpb, w_branch, b_branch, w_o, ln_g, ln_b, w_ff1, w_ff2):
    batch, seq, d = x.shape
    ctx_len = ctx.shape[1]
    depth = w_ada.shape[0]
    rows = seq // GRID_W
    assert d == D_MODEL and seq % (NA_QROWS * GRID_W) == 0 and rows >= NA_KROWS
    alpha = (2 * depth) ** 0.25

    cond_rows = -(-(batch + 1) // 8) * 8
    cond = jnp.zeros((cond_rows, d), F32).at[:batch].set(c).at[batch].set(c_ctx)
    mods = _ada_call(cond, w_ada, b_ada)

    w_in_p = jnp.concatenate([w_in[:, :, MAIN_W:], w_in[:, :, :MAIN_W]], axis=-1).astype(BF16)
    b_in_p = jnp.concatenate([b_in[:, MAIN_W:], b_in[:, :MAIN_W]], axis=-1).reshape(depth, 1, PROJ_W)
    w_branch_b = w_branch.astype(BF16)
    w_o_b = w_o.astype(BF16)
    w_ff1_b = w_ff1.astype(BF16)
    w_ff2_b = w_ff2.astype(BF16)
    rope_tabs = _rope_tables(seq)

    tm_lat = min(2048, seq)
    tq = min(256, seq)
    tk = min(1024, seq)
    na_bias = _na_bias_table(na_rpb, rows)
    m_ctx = batch * ctx_len

    xl = x.reshape(batch * seq, d)
    xc = ctx.reshape(m_ctx, d)
    for l in range(depth):
        last = l == depth - 1
        lam_init = 0.8 - 0.6 * math.exp(-0.3 * l)
        mod_lat = mods[l, :batch].reshape(batch, 6, d)
        mod_ctx = mods[l, batch:batch + 1].reshape(1, 6, d)
        local_w = (conv_a_w[l], conv_a_b[l], conf_dw_w[l], conf_dw_b[l], conf_ln_g[l], conf_ln_b[l])

        pc = _proj_call(xc, mod_ctx, w_in_p[l], b_in_p[l], None, m_ctx, m_ctx)
        p = _proj_call(xl, mod_lat, w_in_p[l], b_in_p[l], rope_tabs, seq, tm_lat)

        y_a, y_b = _local_call(p, *local_w, seq, min(512, seq))
        y_c = _diff_call(diff_lambda[l], p, pc, p, diff_norm_g[l], lam_init, batch, seq, ctx_len, seq, tq, tk)
        y_d = _na_lat_call(p, pc, na_bias[l], batch, seq, ctx_len)
        x1 = _merge_call(xl, mod_lat, p, (y_a, y_b, y_c, y_d), w_branch_b[l], b_branch[l], w_o_b[l],
                         ln_g[l, 0], ln_b[l, 0], seq, min(512, seq), alpha)
        xl = _ffn_call(x1, mod_lat, w_ff1_b[l], w_ff2_b[l], ln_g[l, 1], ln_b[l, 1], seq, min(1024, seq), 1024, alpha)

        if not last:
            yc_a, yc_b = _local_call(pc, *local_w, ctx_len, ctx_len)
            yc_c = _diff_call(diff_lambda[l], pc, pc, None, diff_norm_g[l], lam_init, batch, ctx_len, ctx_len, 0,
                              ctx_len, tk)
            yc_d = _na_ctx_call(pc, batch, ctx_len)
            xc1 = _merge_call(xc, mod_ctx, pc, (yc_a, yc_b, yc_c, yc_d), w_branch_b[l], b_branch[l], w_o_b[l],
                              ln_g[l, 0], ln_b[l, 0], m_ctx, min(512, m_ctx), alpha)
            xc = _ffn_call(xc1, mod_ctx, w_ff1_b[l], w_ff2_b[l], ln_g[l, 1], ln_b[l, 1], m_ctx, min(1024, m_ctx),
                           512, alpha)
    return xl.reshape(batch, seq, d)
```

```python
import functools
import math

import jax
import jax.numpy as jnp
from jax import lax
from jax.experimental import pallas as pl
from jax.experimental.pallas import tpu as pltpu

F32 = jnp.float32
BF16 = jnp.bfloat16

D_MODEL = 1024
GRID_W = 64
BRANCH_W = D_MODEL // 2
N_BRANCH = 4
CONV_A_K = 3
CONF_K = 31
DIFF_HEADS = 4
DIFF_D = D_MODEL // 16
DIFF_V = 2 * DIFF_D
NA_HEADS = 8
NA_D = D_MODEL // 16
NA_WIN_R = 8
NA_WIN_C = 16
D_FF = 4 * D_MODEL
ROPE_BASE = 10000.0
LN_EPS = 1e-5
NEG_INF = -1e30

GATE_W = N_BRANCH * D_MODEL
MAIN_W = 11 * BRANCH_W
PROJ_W = GATE_W + MAIN_W
PROJ_TN = 512
T_GATE_END = GATE_W // PROJ_TN
T_AB, T_AC, T_AX, T_CA, T_CG, T_DQ, T_DK, T_DV, T_NQ, T_NK, T_NV = range(T_GATE_END, T_GATE_END + 11)
LANES = 128
C_DQ, C_DK, C_DV, C_NQ, C_NK, C_NV = (t * (PROJ_TN // LANES) for t in (T_DQ, T_DK, T_DV, T_NQ, T_NK, T_NV))

HALO = 16
SUBLANES = 8
Z_SHIFTS = tuple(sorted({(HALO - CONV_A_K // 2 + k) % SUBLANES for k in range(CONV_A_K)}))
NA_QROWS = 8
NA_KROWS = 16
NA_HPS = 8
NA_KBLKS = 4
VMEM_LIMIT = 56 * 1024 * 1024
Q_SCALE = DIFF_D ** -0.5 * math.log2(math.e)


def _params(*sem):
    return pltpu.CompilerParams(dimension_semantics=sem, vmem_limit_bytes=VMEM_LIMIT)


def _sigmoid(v):
    return 1.0 / (1.0 + jnp.exp(-v))


def _layer_norm(v, g, b):
    mu = jnp.mean(v, axis=-1, keepdims=True)
    d = v - mu
    var = jnp.mean(d * d, axis=-1, keepdims=True)
    return d * lax.rsqrt(var + LN_EPS) * g + b


def _ada_kernel(cond_ref, w_ref, b_ref, o_ref):
    cnd = cond_ref[...]
    act = (cnd * _sigmoid(cnd)).astype(BF16)
    o_ref[0] = jnp.dot(act, w_ref[0].astype(BF16), preferred_element_type=F32) + b_ref[0]


def _ada_call(cond, w_ada, b_ada):
    depth, d, n = w_ada.shape
    rows = cond.shape[0]
    tn = 1024
    return pl.pallas_call(
        _ada_kernel,
        grid=(depth, n // tn),
        in_specs=[
            pl.BlockSpec((rows, d), lambda l, j: (0, 0)),
            pl.BlockSpec((1, d, tn), lambda l, j: (l, 0, j)),
            pl.BlockSpec((1, 1, tn), lambda l, j: (l, 0, j)),
        ],
        out_specs=pl.BlockSpec((1, rows, tn), lambda l, j: (l, 0, j)),
        out_shape=jax.ShapeDtypeStruct((depth, rows, n), F32),
        compiler_params=_params("parallel", "parallel"),
        name="ada_mod",
    )(cond, w_ada, b_ada.reshape(depth, 1, n))


def _proj_kernel(*refs, rope):
    if rope:
        x_ref, mod_ref, w_ref, b_ref, cos_ref, sinm_ref, sinp_ref, o_ref, u_ref = refs
    else:
        x_ref, mod_ref, w_ref, b_ref, o_ref, u_ref = refs
    n = pl.program_id(1)

    @pl.when(n == 0)
    def _():
        sh = mod_ref[0, 0:1, :]
        sc = mod_ref[0, 1:2, :]
        u_ref[...] = (x_ref[...] * (1.0 + sc) + sh).astype(BF16)

    is_q = jnp.logical_or(n == T_DQ, n == T_NQ)
    scale = jnp.where(is_q, Q_SCALE, 1.0).astype(F32)
    half = u_ref.shape[0] // 2
    for hf in range(2):
        rs = slice(hf * half, (hf + 1) * half)
        acc = jnp.dot(u_ref[rs, :], w_ref[...], preferred_element_type=F32) + b_ref[...]
        o_ref[rs, :] = (acc * scale).astype(BF16)
    if rope:
        @pl.when(jnp.logical_or(n == T_DQ, n == T_DK))
        def _():
            for hf in range(2):
                rs = slice(hf * half, (hf + 1) * half)
                acc = jnp.dot(u_ref[rs, :], w_ref[...], preferred_element_type=F32) + b_ref[...]
                cs = cos_ref[rs, :]
                sm = sinm_ref[rs, :]
                sp = sinp_ref[rs, :]
                for g in range(PROJ_TN // LANES):
                    xg = acc[:, g * LANES:(g + 1) * LANES]
                    rot = (xg * cs + pltpu.roll(xg, LANES - DIFF_D // 4, 1) * sm
                           + pltpu.roll(xg, DIFF_D // 4, 1) * sp)
                    o_ref[rs, g * LANES:(g + 1) * LANES] = (rot * scale).astype(BF16)


def _proj_call(x, mod, w, b, rope_tabs, seq, tm):
    m, d = x.shape
    ntile = PROJ_W // PROJ_TN
    tpm = seq // tm
    rope = rope_tabs is not None
    in_specs = [
        pl.BlockSpec((tm, d), lambda i, n: (i, 0)),
        pl.BlockSpec((1, 6, d), lambda i, n: (i // tpm, 0, 0)),
        pl.BlockSpec((d, PROJ_TN), lambda i, n: (0, n)),
        pl.BlockSpec((1, PROJ_TN), lambda i, n: (0, n)),
    ]
    args = [x, mod, w, b]
    if rope:
        in_specs += [pl.BlockSpec((tm, LANES), lambda i, n: (i % tpm, 0))] * 3
        args += list(rope_tabs)
    return pl.pallas_call(
        functools.partial(_proj_kernel, rope=rope),
        grid=(m // tm, ntile),
        in_specs=in_specs,
        out_specs=pl.BlockSpec((tm, PROJ_TN), lambda i, n: (i, n)),
        out_shape=jax.ShapeDtypeStruct((m, PROJ_W), BF16),
        scratch_shapes=[pltpu.VMEM((tm, d), BF16)],
        compiler_params=_params("parallel", "arbitrary"),
        name="proj_lat" if rope else "proj_ctx",
    )(*args)


def _local_kernel(ab, ac, ax, ca, cg, acp, axp, cap, cgp, acn, axn, can, cgn,
                  caw, cab, dww, dwb, lng, lnb, ya_ref, yb_ref, z_ref, glu_ref, conv_ref, *, tt, tps):
    i = pl.program_id(0)
    si = i % tps
    has_prev = (si > 0).astype(F32)
    has_next = (si < tps - 1).astype(F32)
    n_ext = tt + 2 * HALO

    def glu(a, g):
        return a[...].astype(F32) * _sigmoid(g[...].astype(F32))

    def prod(a, b):
        return a[...].astype(F32) * b[...].astype(F32)

    ngrp = BRANCH_W // LANES
    for dst, pieces in ((glu_ref, (glu(cap, cgp) * has_prev, glu(ca, cg), glu(can, cgn) * has_next)),
                        (z_ref, (prod(acp, axp) * has_prev, prod(ac, ax), prod(acn, axn) * has_next))):
        for g in range(ngrp):
            ls = slice(g * LANES, (g + 1) * LANES)
            dst[0, g, 0:HALO, :] = pieces[0][:, ls]
            dst[0, g, HALO:HALO + tt, :] = pieces[1][:, ls]
            dst[0, g, HALO + tt:n_ext, :] = pieces[2][:, ls]
    for g in range(ngrp):
        for sft in range(1, SUBLANES):
            glu_ref[sft, g, 0:n_ext - SUBLANES, :] = glu_ref[0, g, pl.ds(sft, n_ext - SUBLANES), :]
        for idx, sft in enumerate(Z_SHIFTS[1:], start=1):
            z_ref[idx, g, 0:n_ext - SUBLANES, :] = z_ref[0, g, pl.ds(sft, n_ext - SUBLANES), :]

    rc = min(tt, 128)
    for g in range(ngrp):
        ls = slice(g * LANES, (g + 1) * LANES)

        def chunk(r, carry):
            r0 = pl.multiple_of(r * rc, rc)
            acc = None
            for k in range(CONV_A_K):
                o = HALO - CONV_A_K // 2 + k
                tap = z_ref[Z_SHIFTS.index(o % SUBLANES), g, pl.ds(r0 + (o - o % SUBLANES), rc), :] * caw[k:k + 1, ls]
                acc = tap if acc is None else acc + tap
            ya = ab[pl.ds(r0, rc), ls].astype(F32) * (acc + cab[:, ls])
            ya_ref[pl.ds(r0, rc), ls] = ya.astype(BF16)

            acc = None
            for k in range(CONF_K):
                o = HALO - CONF_K // 2 + k
                tap = glu_ref[o % SUBLANES, g, pl.ds(r0 + (o - o % SUBLANES), rc), :] * dww[k:k + 1, ls]
                acc = tap if acc is None else acc + tap
            conv_ref[pl.ds(r0, rc), ls] = acc + dwb[:, ls]
            return carry

        lax.fori_loop(0, tt // rc, chunk, 0)

    yn = _layer_norm(conv_ref[...], lng[...], lnb[...])
    yb_ref[...] = (yn * _sigmoid(yn)).astype(BF16)


def _local_call(p, caw, cab, dww, dwb, lng, lnb, seq, tt):
    m = p.shape[0]
    tps = seq // tt
    hb = tt // HALO
    last = m // HALO - 1

    def cur(t):
        return pl.BlockSpec((tt, BRANCH_W), lambda i: (i, t))

    def prev(t):
        return pl.BlockSpec((HALO, BRANCH_W), lambda i: (jnp.maximum(i * hb - 1, 0), t))

    def nxt(t):
        return pl.BlockSpec((HALO, BRANCH_W), lambda i: (jnp.minimum((i + 1) * hb, last), t))

    def full(a):
        return pl.BlockSpec(a.shape, lambda i: (0,) * a.ndim)

    small = [caw, cab.reshape(1, -1), dww, dwb.reshape(1, -1), lng.reshape(1, -1), lnb.reshape(1, -1)]
    in_specs = ([cur(t) for t in (T_AB, T_AC, T_AX, T_CA, T_CG)]
                + [prev(t) for t in (T_AC, T_AX, T_CA, T_CG)]
                + [nxt(t) for t in (T_AC, T_AX, T_CA, T_CG)]
                + [full(a) for a in small])
    out = jax.ShapeDtypeStruct((m, BRANCH_W), BF16)
    return pl.pallas_call(
        functools.partial(_local_kernel, tt=tt, tps=tps),
        grid=(m // tt,),
        in_specs=in_specs,
        out_specs=[pl.BlockSpec((tt, BRANCH_W), lambda i: (i, 0))] * 2,
        out_shape=[out, out],
        scratch_shapes=[pltpu.VMEM((len(Z_SHIFTS), BRANCH_W // LANES, tt + 2 * HALO + SUBLANES, LANES), F32),
                        pltpu.VMEM((SUBLANES, BRANCH_W // LANES, tt + 2 * HALO + SUBLANES, LANES), F32),
                        pltpu.VMEM((tt, BRANCH_W), F32)],
        compiler_params=_params("parallel"),
        name="local_branches",
    )(*([p] * 13), *small)


def _diff_kernel(*refs, tq, tk, n_lat, lam_init):
    if n_lat:
        dl_ref, q_ref, kc_ref, vc_ref, k_ref, v_ref, g_ref, o_ref, m_ref, acc_ref, vca_ref, va_ref = refs
    else:
        dl_ref, q_ref, kc_ref, vc_ref, g_ref, o_ref, m_ref, acc_ref, vca_ref = refs
    dl = dl_ref[...]
    lam = (jnp.exp(jnp.sum(dl[0:1] * dl[1:2], axis=-1, keepdims=True))
           - jnp.exp(jnp.sum(dl[2:3] * dl[3:4], axis=-1, keepdims=True)) + lam_init)

    @pl.when(pl.program_id(2) == 0)
    def _():
        vca_ref[:, :DIFF_V] = vc_ref[...]
        vca_ref[:, DIFF_V:] = jnp.ones((vca_ref.shape[0], DIFF_V), BF16)
        if n_lat:
            va_ref[:, :DIFF_V] = v_ref[...]
            va_ref[:, DIFF_V:] = jnp.ones((va_ref.shape[0], DIFF_V), BF16)

    qf = q_ref[...].astype(F32)
    lane = lax.broadcasted_iota(jnp.int32, qf.shape, 1)
    q2 = jnp.concatenate([jnp.where(lane < DIFF_D, qf, 0.0), jnp.where(lane >= DIFF_D, qf, 0.0)],
                         axis=0).astype(BF16)

    m_ref[...] = jnp.full(m_ref.shape, NEG_INF, F32)
    acc_ref[...] = jnp.zeros(acc_ref.shape, F32)

    def step(kb, vb):
        nc = kb.shape[0] // LANES
        s = lax.dot_general(q2, kb, (((1,), (1,)), ((), ())), preferred_element_type=F32)
        cols = [s[:, c * LANES:(c + 1) * LANES] for c in range(nc)]
        cmax = cols[0]
        for sc in cols[1:]:
            cmax = jnp.maximum(cmax, sc)
        m_old = m_ref[...]
        m_new = jnp.maximum(m_old, jnp.max(cmax, axis=1, keepdims=True))
        alpha = jnp.exp2(m_old - m_new)
        p = jnp.concatenate([jnp.exp2(sc - m_new) for sc in cols], axis=1).astype(BF16)
        pv = jnp.dot(p, vb, preferred_element_type=F32)
        acc_ref[...] = jnp.concatenate([alpha, alpha], axis=1) * acc_ref[...] + pv
        m_ref[...] = m_new

    step(kc_ref[...], vca_ref[...])
    if n_lat:
        def body(j, carry):
            j0 = pl.multiple_of(j * tk, tk)
            step(k_ref[pl.ds(j0, tk), :], va_ref[pl.ds(j0, tk), :])
            return carry
        nb = n_lat // tk
        lax.fori_loop(0, nb - 1, body, 0, unroll=True)
        for j0 in range((nb - 1) * tk, n_lat, tk // 2):
            step(k_ref[j0:j0 + tk // 2, :], va_ref[j0:j0 + tk // 2, :])

    acc = acc_ref[...]
    o = acc[:, :DIFF_V] / acc[:, DIFF_V:]
    od = o[:tq] - lam * o[tq:]
    ms = jnp.mean(od * od, axis=-1, keepdims=True)
    o_ref[...] = (od * lax.rsqrt(ms + LN_EPS) * g_ref[...] * (1.0 - lam_init)).astype(BF16)


def _diff_call(dl, pq, pc, pk, g, lam_init, batch, q_seq, ctx_len, lat_seq, tq, tk):
    nq = q_seq // tq
    in_specs = [
        pl.BlockSpec(dl.shape, lambda b, h, i: (0, 0)),
        pl.BlockSpec((tq, LANES), lambda b, h, i: (b * nq + i, C_DQ + h)),
        pl.BlockSpec((ctx_len, LANES), lambda b, h, i: (b, C_DK + h)),
        pl.BlockSpec((ctx_len, LANES), lambda b, h, i: (b, C_DV + h)),
    ]
    args = [dl, pq, pc, pc]
    if pk is not None:
        in_specs += [pl.BlockSpec((lat_seq, LANES), lambda b, h, i: (b, C_DK + h)),
                     pl.BlockSpec((lat_seq, LANES), lambda b, h, i: (b, C_DV + h))]
        args += [pk, pk]
    in_specs.append(pl.BlockSpec((1, DIFF_V), lambda b, h, i: (0, 0)))
    args.append(g.reshape(1, DIFF_V))
    return pl.pallas_call(
        functools.partial(_diff_kernel, tq=tq, tk=tk, n_lat=lat_seq if pk is not None else 0, lam_init=lam_init),
        grid=(batch, DIFF_HEADS, nq),
        in_specs=in_specs,
        out_specs=pl.BlockSpec((tq, DIFF_V), lambda b, h, i: (b * nq + i, h)),
        out_shape=jax.ShapeDtypeStruct((batch * q_seq, BRANCH_W), BF16),
        scratch_shapes=([pltpu.VMEM((2 * tq, LANES), F32), pltpu.VMEM((2 * tq, 2 * DIFF_V), F32),
                         pltpu.VMEM((ctx_len, 2 * DIFF_V), BF16)]
                        + ([pltpu.VMEM((lat_seq, 2 * DIFF_V), BF16)] if pk is not None else [])),
        compiler_params=_params("arbitrary", "arbitrary", "arbitrary"),
        name="diff_attn_lat" if pk is not None else "diff_attn_ctx",
    )(*args)


def _na_kernel(*refs, lat):
    if lat:
        q_ref, kc_ref, vc_ref = refs[:3]
        k_refs = refs[3:3 + NA_KBLKS]
        v_refs = refs[3 + NA_KBLKS:3 + 2 * NA_KBLKS]
        bias_ref, o_ref = refs[3 + 2 * NA_KBLKS:]
    else:
        q_ref, kc_ref, vc_ref, o_ref = refs
    nt = (((1,), (1,)), ((), ()))
    for pp in range(q_ref.shape[1] // LANES):
        ls = slice(pp * LANES, (pp + 1) * LANES)
        qf = q_ref[:, ls].astype(F32)
        lane = lax.broadcasted_iota(jnp.int32, qf.shape, 1)
        kc = kc_ref[:, ls]
        vc = vc_ref[:, ls]
        if lat:
            kw = jnp.concatenate([r[:, ls] for r in k_refs], axis=0)
            vw = jnp.concatenate([r[:, ls] for r in v_refs], axis=0)
        outs = []
        for hh in range(2):
            sel = (lane < NA_D) if hh == 0 else (lane >= NA_D)
            qz = jnp.where(sel, qf, 0.0).astype(BF16)
            s_c = lax.dot_general(qz, kc, nt, preferred_element_type=F32)
            m = jnp.max(s_c, axis=-1, keepdims=True)
            if lat:
                s_l = (lax.dot_general(qz, kw, nt, preferred_element_type=F32)
                       + bias_ref[0, 2 * pp + hh].astype(F32))
                m = jnp.maximum(m, jnp.max(s_l, axis=-1, keepdims=True))
            p_c = jnp.exp2(s_c - m)
            l = jnp.sum(p_c, axis=-1, keepdims=True)
            o = jnp.dot(p_c.astype(BF16), vc, preferred_element_type=F32)
            if lat:
                p_l = jnp.exp2(s_l - m)
                l = l + jnp.sum(p_l, axis=-1, keepdims=True)
                o = o + jnp.dot(p_l.astype(BF16), vw, preferred_element_type=F32)
            outs.append(o / l)
        o_ref[:, ls] = jnp.where(lane < NA_D, outs[0], outs[1]).astype(BF16)


def _na_bias_plan(rows):
    win_r = NA_WIN_R
    plan = []
    for r_b in (0, NA_QROWS, rows - NA_QROWS):
        kw0 = min(max(r_b - win_r // 2, 0), rows - NA_KROWS)
        per_q = []
        for q in range(NA_QROWS):
            qr = r_b + q
            r0 = min(max(qr - win_r // 2, 0), rows - win_r)
            per_kp = []
            for kp in range(NA_KROWS // 2):
                kl = kw0 + 2 * kp
                ok_l = r0 <= kl < r0 + win_r
                ok_r = r0 <= kl + 1 < r0 + win_r
                dr_l = kl - qr + win_r - 1
                if ok_l and ok_r:
                    per_kp.append((0, dr_l + 1))
                elif ok_l:
                    per_kp.append((1, dr_l))
                elif ok_r:
                    per_kp.append((2, dr_l + 1))
                else:
                    per_kp.append((1, 2 * win_r - 1))
            per_q.append(per_kp)
        plan.append(per_q)
    return plan


def _na_pair_blocks(rpb):
    hi = lax.Precision.HIGHEST
    cidx = jnp.arange(GRID_W)
    cstart = jnp.clip(cidx - NA_WIN_C // 2, 0, GRID_W - NA_WIN_C)
    col_ok = (cidx[None, :] >= cstart[:, None]) & (cidx[None, :] < cstart[:, None] + NA_WIN_C)
    dc = cidx[None, :] - cidx[:, None] + NA_WIN_C - 1
    oh_c = (dc[:, :, None] == jnp.arange(2 * NA_WIN_C - 1)).astype(F32)
    tc = jnp.einsum('lhrc,xyc->lhrxy', rpb.astype(F32), oh_c, precision=hi)
    tc = jnp.where(col_ok, tc * math.log2(math.e), NEG_INF)
    neg = jnp.full(tc.shape[:2] + (1,) + tc.shape[3:], NEG_INF, F32)
    blk = jnp.concatenate([tc, neg], axis=2)
    prev = jnp.concatenate([neg, tc], axis=2)
    negs = jnp.full(blk.shape, NEG_INF, F32)
    out = jnp.stack([jnp.concatenate([prev, blk], axis=-1), jnp.concatenate([blk, negs], axis=-1),
                     jnp.concatenate([negs, blk], axis=-1)], axis=2)
    return out.astype(BF16)


def _na_bias_kernel(pt_ref, o_ref, *, plan):
    for p, per_q in enumerate(plan):
        for q, per_kp in enumerate(per_q):
            for kp, (var, j) in enumerate(per_kp):
                o_ref[0, p, 0, q * GRID_W:(q + 1) * GRID_W, 2 * kp * GRID_W:2 * (kp + 1) * GRID_W] = pt_ref[0, 0, var, j]


def _na_bias_table(rpb, rows):
    depth, heads = rpb.shape[:2]
    pt = _na_pair_blocks(rpb)
    tq, tkw = NA_QROWS * GRID_W, NA_KROWS * GRID_W
    return pl.pallas_call(
        functools.partial(_na_bias_kernel, plan=_na_bias_plan(rows)),
        grid=(depth, heads),
        in_specs=[pl.BlockSpec((1, 1) + pt.shape[2:], lambda l, h: (l, h, 0, 0, 0, 0))],
        out_specs=pl.BlockSpec((1, 3, 1, tq, tkw), lambda l, h: (l, 0, h, 0, 0)),
        out_shape=jax.ShapeDtypeStruct((depth, 3, heads, tq, tkw), BF16),
        compiler_params=_params("parallel", "parallel"),
        name="na_bias_table",
    )(pt)


def _na_lat_call(p, pc, bias, batch, seq, ctx_len):
    rows = seq // GRID_W
    tq = NA_QROWS * GRID_W
    nblk = seq // tq
    w = NA_HPS * NA_D
    cq, ck, cv = (c * LANES // w for c in (C_NQ, C_NK, C_NV))

    def pat(i):
        return jnp.where(i == 0, 0, jnp.where(i == nblk - 1, 2, 1))

    kblk = NA_KROWS * GRID_W // NA_KBLKS

    def kwb(i):
        return jnp.clip(i * NA_QROWS - NA_WIN_R // 2, 0, rows - NA_KROWS) * GRID_W // kblk

    return pl.pallas_call(
        functools.partial(_na_kernel, lat=True),
        grid=(batch, NA_HEADS // NA_HPS, nblk),
        in_specs=[
            pl.BlockSpec((tq, w), lambda b, h, i: (b * nblk + i, cq + h)),
            pl.BlockSpec((ctx_len, w), lambda b, h, i: (b, ck + h)),
            pl.BlockSpec((ctx_len, w), lambda b, h, i: (b, cv + h)),
            *[pl.BlockSpec((kblk, w), functools.partial(lambda b, h, i, j, c: (b * (seq // kblk) + kwb(i) + j, c + h),
                                                        j=j, c=ck)) for j in range(NA_KBLKS)],
            *[pl.BlockSpec((kblk, w), functools.partial(lambda b, h, i, j, c: (b * (seq // kblk) + kwb(i) + j, c + h),
                                                        j=j, c=cv)) for j in range(NA_KBLKS)],
            pl.BlockSpec((1, NA_HPS, tq, NA_KROWS * GRID_W), lambda b, h, i: (pat(i), h, 0, 0)),
        ],
        out_specs=pl.BlockSpec((tq, w), lambda b, h, i: (b * nblk + i, h)),
        out_shape=jax.ShapeDtypeStruct((batch * seq, BRANCH_W), BF16),
        compiler_params=_params("parallel", "parallel", "arbitrary"),
        name="na_attn_lat",
    )(p, pc, pc, *([p] * (2 * NA_KBLKS)), bias)


def _na_ctx_call(pc, batch, ctx_len):
    return pl.pallas_call(
        functools.partial(_na_kernel, lat=False),
        grid=(batch, NA_HEADS // 2),
        in_specs=[
            pl.BlockSpec((ctx_len, LANES), lambda b, h: (b, C_NQ + h)),
            pl.BlockSpec((ctx_len, LANES), lambda b, h: (b, C_NK + h)),
            pl.BlockSpec((ctx_len, LANES), lambda b, h: (b, C_NV + h)),
        ],
        out_specs=pl.BlockSpec((ctx_len, LANES), lambda b, h: (b, h)),
        out_shape=jax.ShapeDtypeStruct((batch * ctx_len, BRANCH_W), BF16),
        compiler_params=_params("parallel", "parallel"),
        name="na_attn_ctx",
    )(pc, pc, pc)


def _merge_kernel(x_ref, mod_ref, g0, g1, g2, g3, y0, y1, y2, y3, wb_ref, bb_ref, wo_ref, lng, lnb, o_ref, *, alpha):
    merged = None
    for idx, (g_ref, y_ref) in enumerate(((g0, y0), (g1, y1), (g2, y2), (g3, y3))):
        br = jnp.dot(y_ref[...], wb_ref[idx], preferred_element_type=F32) + bb_ref[idx:idx + 1, :]
        term = _sigmoid(g_ref[...].astype(F32)) * br
        merged = term if merged is None else merged + term
    y = jnp.dot(merged.astype(BF16), wo_ref[...], preferred_element_type=F32)
    gate = mod_ref[0, 2:3, :]
    o_ref[...] = _layer_norm(alpha * x_ref[...] + gate * y, lng[...], lnb[...])


def _merge_call(x, mod, p, ys, wb, bb, wo, lng, lnb, seq, tm, alpha):
    m, d = x.shape
    tpm = seq // tm
    in_specs = ([pl.BlockSpec((tm, d), lambda i: (i, 0)),
                 pl.BlockSpec((1, 6, d), lambda i: (i // tpm, 0, 0))]
                + [pl.BlockSpec((tm, d), functools.partial(lambda i, k: (i, k), k=k)) for k in range(N_BRANCH)]
                + [pl.BlockSpec((tm, BRANCH_W), lambda i: (i, 0))] * N_BRANCH
                + [pl.BlockSpec(wb.shape, lambda i: (0, 0, 0)),
                   pl.BlockSpec(bb.shape, lambda i: (0, 0)),
                   pl.BlockSpec(wo.shape, lambda i: (0, 0)),
                   pl.BlockSpec((1, d), lambda i: (0, 0)),
                   pl.BlockSpec((1, d), lambda i: (0, 0))])
    return pl.pallas_call(
        functools.partial(_merge_kernel, alpha=alpha),
        grid=(m // tm,),
        in_specs=in_specs,
        out_specs=pl.BlockSpec((tm, d), lambda i: (i, 0)),
        out_shape=jax.ShapeDtypeStruct((m, d), F32),
        compiler_params=_params("parallel"),
        name="merge_out",
    )(x, mod, p, p, p, p, *ys, wb, bb, wo, lng.reshape(1, d), lnb.reshape(1, d))


def _ffn_kernel(x_ref, mod_ref, w1_ref, w2_ref, lng, lnb, o_ref, h_ref, acc_ref, *, alpha):
    k = pl.program_id(1)

    @pl.when(k == 0)
    def _():
        sh = mod_ref[0, 3:4, :]
        sc = mod_ref[0, 4:5, :]
        h_ref[...] = (x_ref[...] * (1.0 + sc) + sh).astype(BF16)
        acc_ref[...] = jnp.zeros(acc_ref.shape, F32)

    a = jnp.maximum(jnp.dot(h_ref[...], w1_ref[...], preferred_element_type=F32), 0.0)
    acc_ref[...] += jnp.dot((a * a).astype(BF16), w2_ref[...], preferred_element_type=F32)

    @pl.when(k == pl.num_programs(1) - 1)
    def _():
        gate = mod_ref[0, 5:6, :]
        o_ref[...] = _layer_norm(alpha * x_ref[...] + gate * acc_ref[...], lng[...], lnb[...])


def _ffn_call(x, mod, w1, w2, lng, lnb, seq, tm, kf, alpha):
    m, d = x.shape
    tpm = seq // tm
    return pl.pallas_call(
        functools.partial(_ffn_kernel, alpha=alpha),
        grid=(m // tm, D_FF // kf),
        in_specs=[
            pl.BlockSpec((tm, d), lambda i, k: (i, 0)),
            pl.BlockSpec((1, 6, d), lambda i, k: (i // tpm, 0, 0)),
            pl.BlockSpec((d, kf), lambda i, k: (0, k)),
            pl.BlockSpec((kf, d), lambda i, k: (k, 0)),
            pl.BlockSpec((1, d), lambda i, k: (0, 0)),
            pl.BlockSpec((1, d), lambda i, k: (0, 0)),
        ],
        out_specs=pl.BlockSpec((tm, d), lambda i, k: (i, 0)),
        out_shape=jax.ShapeDtypeStruct((m, d), F32),
        scratch_shapes=[pltpu.VMEM((tm, d), BF16), pltpu.VMEM((tm, d), F32)],
        compiler_params=_params("parallel", "arbitrary"),
        name="ffn",
    )(x, mod, w1, w2, lng.reshape(1, d), lnb.reshape(1, d))


def _rope_tables(seq):
    nf = DIFF_D // 4
    t = jnp.arange(seq)
    freqs = jnp.power(ROPE_BASE, -jnp.arange(nf, dtype=F32) / nf)
    pos = jnp.stack([t // GRID_W, t % GRID_W], axis=-1).astype(F32)
    ang = pos[:, :, None] * freqs
    cos, sin = jnp.cos(ang), jnp.sin(ang)
    shape = (seq, LANES // DIFF_D, 2, 2, nf)
    cos_t = jnp.broadcast_to(cos[:, None, :, None, :], shape)
    sin_t = jnp.broadcast_to(sin[:, None, :, None, :], shape)
    half = jnp.arange(2)[None, None, None, :, None]
    sin_m = jnp.where(half == 0, -sin_t, 0.0)
    sin_p = jnp.where(half == 1, sin_t, 0.0)
    return tuple(a.reshape(seq, LANES) for a in (cos_t, sin_m, sin_p))


def kernel(x, c, ctx, c_ctx, w_ada, b_ada, w_in, b_in, conv_a_w, conv_a_b, conf_dw_w, conf_dw_b, conf_ln_g, conf_ln_b, diff_lambda, diff_norm_g, na_rpb, w_branch, b_branch, w_o, ln_g, ln_b, w_ff1, w_ff2):
    batch, seq, d = x.shape
    ctx_len = ctx.shape[1]
    depth = w_ada.shape[0]
    rows = seq // GRID_W
    assert d == D_MODEL and seq % (NA_QROWS * GRID_W) == 0 and rows >= NA_KROWS
    alpha = (2 * depth) ** 0.25

    cond_rows = -(-(batch + 1) // 8) * 8
    cond = jnp.zeros((cond_rows, d), F32).at[:batch].set(c).at[batch].set(c_ctx)
    mods = _ada_call(cond, w_ada, b_ada)

    w_in_p = jnp.concatenate([w_in[:, :, MAIN_W:], w_in[:, :, :MAIN_W]], axis=-1).astype(BF16)
    b_in_p = jnp.concatenate([b_in[:, MAIN_W:], b_in[:, :MAIN_W]], axis=-1).reshape(depth, 1, PROJ_W)
    w_branch_b = w_branch.astype(BF16)
    w_o_b = w_o.astype(BF16)
    w_ff1_b = w_ff1.astype(BF16)
    w_ff2_b = w_ff2.astype(BF16)
    rope_tabs = _rope_tables(seq)

    tm_lat = min(2048, seq)
    tq = min(512, seq)
    tk = min(1024, seq)
    na_bias = _na_bias_table(na_rpb, rows)
    m_ctx = batch * ctx_len

    xl = x.reshape(batch * seq, d)
    xc = ctx.reshape(m_ctx, d)
    for l in range(depth):
        last = l == depth - 1
        lam_init = 0.8 - 0.6 * math.exp(-0.3 * l)
        mod_lat = mods[l, :batch].reshape(batch, 6, d)
        mod_ctx = mods[l, batch:batch + 1].reshape(1, 6, d)
        local_w = (conv_a_w[l], conv_a_b[l], conf_dw_w[l], conf_dw_b[l], conf_ln_g[l], conf_ln_b[l])

        pc = _proj_call(xc, mod_ctx, w_in_p[l], b_in_p[l], None, m_ctx, m_ctx)
        p = _proj_call(xl, mod_lat, w_in_p[l], b_in_p[l], rope_tabs, seq, tm_lat)

        y_a, y_b = _local_call(p, *local_w, seq, min(512, seq))
        y_c = _diff_call(diff_lambda[l], p, pc, p, diff_norm_g[l], lam_init, batch, seq, ctx_len, seq, tq, tk)
        y_d = _na_lat_call(p, pc, na_bias[l], batch, seq, ctx_len)
        x1 = _merge_call(xl, mod_lat, p, (y_a, y_b, y_c, y_d), w_branch_b[l], b_branch[l], w_o_b[l],
                         ln_g[l, 0], ln_b[l, 0], seq, min(512, seq), alpha)
        xl = _ffn_call(x1, mod_lat, w_ff1_b[l], w_ff2_b[l], ln_g[l, 1], ln_b[l, 1], seq, min(1024, seq), 1024, alpha)

        if not last:
            yc_a, yc_b = _local_call(pc, *local_w, ctx_len, ctx_len)
            yc_c = _diff_call(diff_lambda[l], pc, pc, None, diff_norm_g[l], lam_init, batch, ctx_len, ctx_len, 0,
                              ctx_len, tk)
            yc_d = _na_ctx_call(pc, batch, ctx_len)
            xc1 = _merge_call(xc, mod_ctx, pc, (yc_a, yc_b, yc_c, yc_d), w_branch_b[l], b_branch[l], w_o_b[l],
                              ln_g[l, 0], ln_b[l, 0], m_ctx, min(512, m_ctx), alpha)
            xc = _ffn_call(xc1, mod_ctx, w_ff1_b[l], w_ff2_b[l], ln_g[l, 1], ln_b[l, 1], m_ctx, min(1024, m_ctx),
                           512, alpha)
    return xl.reshape(batch, seq, d)
```

```python
import functools
import math

import jax
import jax.numpy as jnp
from jax import lax
from jax.experimental import pallas as pl
from jax.experimental.pallas import tpu as pltpu

F32 = jnp.float32
BF16 = jnp.bfloat16

D_MODEL = 1024
GRID_W = 64
BRANCH_W = D_MODEL // 2
N_BRANCH = 4
CONV_A_K = 3
CONF_K = 31
DIFF_HEADS = 4
DIFF_D = D_MODEL // 16
DIFF_V = 2 * DIFF_D
NA_HEADS = 8
NA_D = D_MODEL // 16
NA_WIN_R = 8
NA_WIN_C = 16
D_FF = 4 * D_MODEL
ROPE_BASE = 10000.0
LN_EPS = 1e-5
NEG_INF = -1e30

GATE_W = N_BRANCH * D_MODEL
MAIN_W = 11 * BRANCH_W
PROJ_W = GATE_W + MAIN_W
PROJ_TN = 512
T_GATE_END = GATE_W // PROJ_TN
T_AB, T_AC, T_AX, T_CA, T_CG, T_DQ, T_DK, T_DV, T_NQ, T_NK, T_NV = range(T_GATE_END, T_GATE_END + 11)
LANES = 128
C_DQ, C_DK, C_DV, C_NQ, C_NK, C_NV = (t * (PROJ_TN // LANES) for t in (T_DQ, T_DK, T_DV, T_NQ, T_NK, T_NV))

HALO = 16
SUBLANES = 8
Z_SHIFTS = tuple(sorted({(HALO - CONV_A_K // 2 + k) % SUBLANES for k in range(CONV_A_K)}))
NA_QROWS = 8
NA_KROWS = 16
NA_HPS = 8
NA_KBLKS = 4
VMEM_LIMIT = 56 * 1024 * 1024
Q_SCALE = DIFF_D ** -0.5 * math.log2(math.e)


def _params(*sem):
    return pltpu.CompilerParams(dimension_semantics=sem, vmem_limit_bytes=VMEM_LIMIT)


def _sigmoid(v):
    return 1.0 / (1.0 + jnp.exp(-v))


def _layer_norm(v, g, b):
    mu = jnp.mean(v, axis=-1, keepdims=True)
    d = v - mu
    var = jnp.mean(d * d, axis=-1, keepdims=True)
    return d * lax.rsqrt(var + LN_EPS) * g + b


def _ada_kernel(cond_ref, w_ref, b_ref, o_ref):
    cnd = cond_ref[...]
    act = (cnd * _sigmoid(cnd)).astype(BF16)
    o_ref[0] = jnp.dot(act, w_ref[0].astype(BF16), preferred_element_type=F32) + b_ref[0]


def _ada_call(cond, w_ada, b_ada):
    depth, d, n = w_ada.shape
    rows = cond.shape[0]
    tn = 1024
    return pl.pallas_call(
        _ada_kernel,
        grid=(depth, n // tn),
        in_specs=[
            pl.BlockSpec((rows, d), lambda l, j: (0, 0)),
            pl.BlockSpec((1, d, tn), lambda l, j: (l, 0, j)),
            pl.BlockSpec((1, 1, tn), lambda l, j: (l, 0, j)),
        ],
        out_specs=pl.BlockSpec((1, rows, tn), lambda l, j: (l, 0, j)),
        out_shape=jax.ShapeDtypeStruct((depth, rows, n), F32),
        compiler_params=_params("parallel", "parallel"),
        name="ada_mod",
    )(cond, w_ada, b_ada.reshape(depth, 1, n))


def _proj_kernel(*refs, rope):
    if rope:
        x_ref, mod_ref, w_ref, b_ref, cos_ref, sinm_ref, sinp_ref, o_ref, u_ref = refs
    else:
        x_ref, mod_ref, w_ref, b_ref, o_ref, u_ref = refs
    n = pl.program_id(1)

    @pl.when(n == 0)
    def _():
        sh = mod_ref[0, 0:1, :]
        sc = mod_ref[0, 1:2, :]
        u_ref[...] = (x_ref[...] * (1.0 + sc) + sh).astype(BF16)

    is_q = jnp.logical_or(n == T_DQ, n == T_NQ)
    scale = jnp.where(is_q, Q_SCALE, 1.0).astype(F32)
    half = u_ref.shape[0] // 2

    def plain():
        for hf in range(2):
            rs = slice(hf * half, (hf + 1) * half)
            acc = jnp.dot(u_ref[rs, :], w_ref[...], preferred_element_type=F32) + b_ref[...]
            o_ref[rs, :] = (acc * scale).astype(BF16)

    if not rope:
        plain()
    else:
        is_rope = jnp.logical_or(n == T_DQ, n == T_DK)
        pl.when(jnp.logical_not(is_rope))(plain)

        @pl.when(is_rope)
        def _():
            for hf in range(2):
                rs = slice(hf * half, (hf + 1) * half)
                acc = jnp.dot(u_ref[rs, :], w_ref[...], preferred_element_type=F32) + b_ref[...]
                cs = cos_ref[rs, :]
                sm = sinm_ref[rs, :]
                sp = sinp_ref[rs, :]
                for g in range(PROJ_TN // LANES):
                    xg = acc[:, g * LANES:(g + 1) * LANES]
                    rot = (xg * cs + pltpu.roll(xg, LANES - DIFF_D // 4, 1) * sm
                           + pltpu.roll(xg, DIFF_D // 4, 1) * sp)
                    o_ref[rs, g * LANES:(g + 1) * LANES] = (rot * scale).astype(BF16)


def _proj_call(x, mod, w, b, rope_tabs, seq, tm):
    m, d = x.shape
    ntile = PROJ_W // PROJ_TN
    tpm = seq // tm
    rope = rope_tabs is not None
    in_specs = [
        pl.BlockSpec((tm, d), lambda i, n: (i, 0)),
        pl.BlockSpec((1, 6, d), lambda i, n: (i // tpm, 0, 0)),
        pl.BlockSpec((d, PROJ_TN), lambda i, n: (0, n)),
        pl.BlockSpec((1, PROJ_TN), lambda i, n: (0, n)),
    ]
    args = [x, mod, w, b]
    if rope:
        in_specs += [pl.BlockSpec((tm, LANES), lambda i, n: (i % tpm, 0))] * 3
        args += list(rope_tabs)
    return pl.pallas_call(
        functools.partial(_proj_kernel, rope=rope),
        grid=(m // tm, ntile),
        in_specs=in_specs,
        out_specs=pl.BlockSpec((tm, PROJ_TN), lambda i, n: (i, n)),
        out_shape=jax.ShapeDtypeStruct((m, PROJ_W), BF16),
        scratch_shapes=[pltpu.VMEM((tm, d), BF16)],
        compiler_params=_params("parallel", "arbitrary"),
        name="proj_lat" if rope else "proj_ctx",
    )(*args)


def _local_kernel(ab, ac, ax, ca, cg, acp, axp, cap, cgp, acn, axn, can, cgn,
                  caw, cab, dww, dwb, lng, lnb, ya_ref, yb_ref, z_ref, glu_ref, conv_ref, *, tt, tps):
    i = pl.program_id(0)
    si = i % tps
    has_prev = (si > 0).astype(F32)
    has_next = (si < tps - 1).astype(F32)
    n_ext = tt + 2 * HALO

    def glu(a, g):
        return a[...].astype(F32) * _sigmoid(g[...].astype(F32))

    def prod(a, b):
        return a[...].astype(F32) * b[...].astype(F32)

    ngrp = BRANCH_W // LANES
    for dst, pieces in ((glu_ref, (glu(cap, cgp) * has_prev, glu(ca, cg), glu(can, cgn) * has_next)),
                        (z_ref, (prod(acp, axp) * has_prev, prod(ac, ax), prod(acn, axn) * has_next))):
        for g in range(ngrp):
            ls = slice(g * LANES, (g + 1) * LANES)
            dst[0, g, 0:HALO, :] = pieces[0][:, ls]
            dst[0, g, HALO:HALO + tt, :] = pieces[1][:, ls]
            dst[0, g, HALO + tt:n_ext, :] = pieces[2][:, ls]
    for g in range(ngrp):
        for sft in range(1, SUBLANES):
            glu_ref[sft, g, 0:n_ext - SUBLANES, :] = glu_ref[0, g, pl.ds(sft, n_ext - SUBLANES), :]
        for idx, sft in enumerate(Z_SHIFTS[1:], start=1):
            z_ref[idx, g, 0:n_ext - SUBLANES, :] = z_ref[0, g, pl.ds(sft, n_ext - SUBLANES), :]

    rc = min(tt, 128)
    for g in range(ngrp):
        ls = slice(g * LANES, (g + 1) * LANES)

        def chunk(r, carry):
            r0 = pl.multiple_of(r * rc, rc)
            acc = None
            for k in range(CONV_A_K):
                o = HALO - CONV_A_K // 2 + k
                tap = z_ref[Z_SHIFTS.index(o % SUBLANES), g, pl.ds(r0 + (o - o % SUBLANES), rc), :] * caw[k:k + 1, ls]
                acc = tap if acc is None else acc + tap
            ya = ab[pl.ds(r0, rc), ls].astype(F32) * (acc + cab[:, ls])
            ya_ref[pl.ds(r0, rc), ls] = ya.astype(BF16)

            acc = None
            for k in range(CONF_K):
                o = HALO - CONF_K // 2 + k
                tap = glu_ref[o % SUBLANES, g, pl.ds(r0 + (o - o % SUBLANES), rc), :] * dww[k:k + 1, ls]
                acc = tap if acc is None else acc + tap
            conv_ref[pl.ds(r0, rc), ls] = acc + dwb[:, ls]
            return carry

        lax.fori_loop(0, tt // rc, chunk, 0)

    yn = _layer_norm(conv_ref[...], lng[...], lnb[...])
    yb_ref[...] = (yn * _sigmoid(yn)).astype(BF16)


def _local_call(p, caw, cab, dww, dwb, lng, lnb, seq, tt):
    m = p.shape[0]
    tps = seq // tt
    hb = tt // HALO
    last = m // HALO - 1

    def cur(t):
        return pl.BlockSpec((tt, BRANCH_W), lambda i: (i, t))

    def prev(t):
        return pl.BlockSpec((HALO, BRANCH_W), lambda i: (jnp.maximum(i * hb - 1, 0), t))

    def nxt(t):
        return pl.BlockSpec((HALO, BRANCH_W), lambda i: (jnp.minimum((i + 1) * hb, last), t))

    def full(a):
        return pl.BlockSpec(a.shape, lambda i: (0,) * a.ndim)

    small = [caw, cab.reshape(1, -1), dww, dwb.reshape(1, -1), lng.reshape(1, -1), lnb.reshape(1, -1)]
    in_specs = ([cur(t) for t in (T_AB, T_AC, T_AX, T_CA, T_CG)]
                + [prev(t) for t in (T_AC, T_AX, T_CA, T_CG)]
                + [nxt(t) for t in (T_AC, T_AX, T_CA, T_CG)]
                + [full(a) for a in small])
    out = jax.ShapeDtypeStruct((m, BRANCH_W), BF16)
    return pl.pallas_call(
        functools.partial(_local_kernel, tt=tt, tps=tps),
        grid=(m // tt,),
        in_specs=in_specs,
        out_specs=[pl.BlockSpec((tt, BRANCH_W), lambda i: (i, 0))] * 2,
        out_shape=[out, out],
        scratch_shapes=[pltpu.VMEM((len(Z_SHIFTS), BRANCH_W // LANES, tt + 2 * HALO + SUBLANES, LANES), F32),
                        pltpu.VMEM((SUBLANES, BRANCH_W // LANES, tt + 2 * HALO + SUBLANES, LANES), F32),
                        pltpu.VMEM((tt, BRANCH_W), F32)],
        compiler_params=_params("parallel"),
        name="local_branches",
    )(*([p] * 13), *small)


def _diff_kernel(*refs, tq, tk, n_lat, lam_init):
    if n_lat:
        dl_ref, q_ref, kc_ref, vc_ref, k_ref, v_ref, g_ref, o_ref, m_ref, acc_ref, vca_ref, va_ref = refs
    else:
        dl_ref, q_ref, kc_ref, vc_ref, g_ref, o_ref, m_ref, acc_ref, vca_ref = refs
    dl = dl_ref[...]
    lam = (jnp.exp(jnp.sum(dl[0:1] * dl[1:2], axis=-1, keepdims=True))
           - jnp.exp(jnp.sum(dl[2:3] * dl[3:4], axis=-1, keepdims=True)) + lam_init)

    @pl.when(pl.program_id(2) == 0)
    def _():
        vca_ref[:, :DIFF_V] = vc_ref[...]
        vca_ref[:, DIFF_V:] = jnp.ones((vca_ref.shape[0], DIFF_V), BF16)
        if n_lat:
            va_ref[:, :DIFF_V] = v_ref[...]
            va_ref[:, DIFF_V:] = jnp.ones((va_ref.shape[0], DIFF_V), BF16)

    qf = q_ref[...].astype(F32)
    lane = lax.broadcasted_iota(jnp.int32, qf.shape, 1)
    q2 = jnp.concatenate([jnp.where(lane < DIFF_D, qf, 0.0), jnp.where(lane >= DIFF_D, qf, 0.0)],
                         axis=0).astype(BF16)

    m_ref[...] = jnp.full(m_ref.shape, NEG_INF, F32)
    acc_ref[...] = jnp.zeros(acc_ref.shape, F32)

    def step(kb, vb):
        nc = kb.shape[0] // LANES
        s = lax.dot_general(q2, kb, (((1,), (1,)), ((), ())), preferred_element_type=F32)
        cols = [s[:, c * LANES:(c + 1) * LANES] for c in range(nc)]
        cmax = cols[0]
        for sc in cols[1:]:
            cmax = jnp.maximum(cmax, sc)
        m_old = m_ref[...]
        m_new = jnp.maximum(m_old, jnp.max(cmax, axis=1, keepdims=True))
        alpha = jnp.exp2(m_old - m_new)
        p = jnp.concatenate([jnp.exp2(sc - m_new) for sc in cols], axis=1).astype(BF16)
        pv = jnp.dot(p, vb, preferred_element_type=F32)
        acc_ref[...] = jnp.concatenate([alpha, alpha], axis=1) * acc_ref[...] + pv
        m_ref[...] = m_new

    step(kc_ref[...], vca_ref[...])
    if n_lat:
        def body(j, carry):
            j0 = pl.multiple_of(j * tk, tk)
            step(k_ref[pl.ds(j0, tk), :], va_ref[pl.ds(j0, tk), :])
            return carry
        nb = n_lat // tk
        lax.fori_loop(0, nb - 1, body, 0, unroll=True)
        for j0 in range((nb - 1) * tk, n_lat, tk // 2):
            step(k_ref[j0:j0 + tk // 2, :], va_ref[j0:j0 + tk // 2, :])

    acc = acc_ref[...]
    o = acc[:, :DIFF_V] / acc[:, DIFF_V:]
    od = o[:tq] - lam * o[tq:]
    ms = jnp.mean(od * od, axis=-1, keepdims=True)
    o_ref[...] = (od * lax.rsqrt(ms + LN_EPS) * g_ref[...] * (1.0 - lam_init)).astype(BF16)


def _diff_call(dl, pq, pc, pk, g, lam_init, batch, q_seq, ctx_len, lat_seq, tq, tk):
    nq = q_seq // tq
    in_specs = [
        pl.BlockSpec(dl.shape, lambda b, h, i: (0, 0)),
        pl.BlockSpec((tq, LANES), lambda b, h, i: (b * nq + i, C_DQ + h)),
        pl.BlockSpec((ctx_len, LANES), lambda b, h, i: (b, C_DK + h)),
        pl.BlockSpec((ctx_len, LANES), lambda b, h, i: (b, C_DV + h)),
    ]
    args = [dl, pq, pc, pc]
    if pk is not None:
        in_specs += [pl.BlockSpec((lat_seq, LANES), lambda b, h, i: (b, C_DK + h)),
                     pl.BlockSpec((lat_seq, LANES), lambda b, h, i: (b, C_DV + h))]
        args += [pk, pk]
    in_specs.append(pl.BlockSpec((1, DIFF_V), lambda b, h, i: (0, 0)))
    args.append(g.reshape(1, DIFF_V))
    return pl.pallas_call(
        functools.partial(_diff_kernel, tq=tq, tk=tk, n_lat=lat_seq if pk is not None else 0, lam_init=lam_init),
        grid=(batch, DIFF_HEADS, nq),
        in_specs=in_specs,
        out_specs=pl.BlockSpec((tq, DIFF_V), lambda b, h, i: (b * nq + i, h)),
        out_shape=jax.ShapeDtypeStruct((batch * q_seq, BRANCH_W), BF16),
        scratch_shapes=([pltpu.VMEM((2 * tq, LANES), F32), pltpu.VMEM((2 * tq, 2 * DIFF_V), F32),
                         pltpu.VMEM((ctx_len, 2 * DIFF_V), BF16)]
                        + ([pltpu.VMEM((lat_seq, 2 * DIFF_V), BF16)] if pk is not None else [])),
        compiler_params=_params("arbitrary", "arbitrary", "arbitrary"),
        name="diff_attn_lat" if pk is not None else "diff_attn_ctx",
    )(*args)


def _na_kernel(*refs, lat):
    if lat:
        q_ref, kc_ref, vc_ref = refs[:3]
        k_refs = refs[3:3 + NA_KBLKS]
        v_refs = refs[3 + NA_KBLKS:3 + 2 * NA_KBLKS]
        bias_ref, o_ref = refs[3 + 2 * NA_KBLKS:]
    else:
        q_ref, kc_ref, vc_ref, o_ref = refs
    nt = (((1,), (1,)), ((), ()))
    for pp in range(q_ref.shape[1] // LANES):
        ls = slice(pp * LANES, (pp + 1) * LANES)
        qf = q_ref[:, ls].astype(F32)
        lane = lax.broadcasted_iota(jnp.int32, qf.shape, 1)
        kc = kc_ref[:, ls]
        vc = vc_ref[:, ls]
        if lat:
            kw = jnp.concatenate([r[:, ls] for r in k_refs], axis=0)
            vw = jnp.concatenate([r[:, ls] for r in v_refs], axis=0)
        outs = []
        for hh in range(2):
            sel = (lane < NA_D) if hh == 0 else (lane >= NA_D)
            qz = jnp.where(sel, qf, 0.0).astype(BF16)
            s_c = lax.dot_general(qz, kc, nt, preferred_element_type=F32)
            m = jnp.max(s_c, axis=-1, keepdims=True)
            if lat:
                s_l = (lax.dot_general(qz, kw, nt, preferred_element_type=F32)
                       + bias_ref[0, 2 * pp + hh].astype(F32))
                m = jnp.maximum(m, jnp.max(s_l, axis=-1, keepdims=True))
            p_c = jnp.exp2(s_c - m)
            l = jnp.sum(p_c, axis=-1, keepdims=True)
            o = jnp.dot(p_c.astype(BF16), vc, preferred_element_type=F32)
            if lat:
                p_l = jnp.exp2(s_l - m)
                l = l + jnp.sum(p_l, axis=-1, keepdims=True)
                o = o + jnp.dot(p_l.astype(BF16), vw, preferred_element_type=F32)
            outs.append(o / l)
        o_ref[:, ls] = jnp.where(lane < NA_D, outs[0], outs[1]).astype(BF16)


def _na_bias_plan(rows):
    win_r = NA_WIN_R
    plan = []
    for r_b in (0, NA_QROWS, rows - NA_QROWS):
        kw0 = min(max(r_b - win_r // 2, 0), rows - NA_KROWS)
        per_q = []
        for q in range(NA_QROWS):
            qr = r_b + q
            r0 = min(max(qr - win_r // 2, 0), rows - win_r)
            per_kp = []
            for kp in range(NA_KROWS // 2):
                kl = kw0 + 2 * kp
                ok_l = r0 <= kl < r0 + win_r
                ok_r = r0 <= kl + 1 < r0 + win_r
                dr_l = kl - qr + win_r - 1
                if ok_l and ok_r:
                    per_kp.append((0, dr_l + 1))
                elif ok_l:
                    per_kp.append((1, dr_l))
                elif ok_r:
                    per_kp.append((2, dr_l + 1))
                else:
                    per_kp.append((1, 2 * win_r - 1))
            per_q.append(per_kp)
        plan.append(per_q)
    return plan


def _na_pair_blocks(rpb):
    hi = lax.Precision.HIGHEST
    cidx = jnp.arange(GRID_W)
    cstart = jnp.clip(cidx - NA_WIN_C // 2, 0, GRID_W - NA_WIN_C)
    col_ok = (cidx[None, :] >= cstart[:, None]) & (cidx[None, :] < cstart[:, None] + NA_WIN_C)
    dc = cidx[None, :] - cidx[:, None] + NA_WIN_C - 1
    oh_c = (dc[:, :, None] == jnp.arange(2 * NA_WIN_C - 1)).astype(F32)
    tc = jnp.einsum('lhrc,xyc->lhrxy', rpb.astype(F32), oh_c, precision=hi)
    tc = jnp.where(col_ok, tc * math.log2(math.e), NEG_INF)
    neg = jnp.full(tc.shape[:2] + (1,) + tc.shape[3:], NEG_INF, F32)
    blk = jnp.concatenate([tc, neg], axis=2)
    prev = jnp.concatenate([neg, tc], axis=2)
    negs = jnp.full(blk.shape, NEG_INF, F32)
    out = jnp.stack([jnp.concatenate([prev, blk], axis=-1), jnp.concatenate([blk, negs], axis=-1),
                     jnp.concatenate([negs, blk], axis=-1)], axis=2)
    return out.astype(BF16)


def _na_bias_kernel(pt_ref, o_ref, *, plan):
    for p, per_q in enumerate(plan):
        for q, per_kp in enumerate(per_q):
            for kp, (var, j) in enumerate(per_kp):
                o_ref[0, p, 0, q * GRID_W:(q + 1) * GRID_W, 2 * kp * GRID_W:2 * (kp + 1) * GRID_W] = pt_ref[0, 0, var, j]


def _na_bias_table(rpb, rows):
    depth, heads = rpb.shape[:2]
    pt = _na_pair_blocks(rpb)
    tq, tkw = NA_QROWS * GRID_W, NA_KROWS * GRID_W
    return pl.pallas_call(
        functools.partial(_na_bias_kernel, plan=_na_bias_plan(rows)),
        grid=(depth, heads),
        in_specs=[pl.BlockSpec((1, 1) + pt.shape[2:], lambda l, h: (l, h, 0, 0, 0, 0))],
        out_specs=pl.BlockSpec((1, 3, 1, tq, tkw), lambda l, h: (l, 0, h, 0, 0)),
        out_shape=jax.ShapeDtypeStruct((depth, 3, heads, tq, tkw), BF16),
        compiler_params=_params("parallel", "parallel"),
        name="na_bias_table",
    )(pt)


def _na_lat_call(p, pc, bias, batch, seq, ctx_len):
    rows = seq // GRID_W
    tq = NA_QROWS * GRID_W
    nblk = seq // tq
    w = NA_HPS * NA_D
    cq, ck, cv = (c * LANES // w for c in (C_NQ, C_NK, C_NV))

    def pat(i):
        return jnp.where(i == 0, 0, jnp.where(i == nblk - 1, 2, 1))

    kblk = NA_KROWS * GRID_W // NA_KBLKS

    def kwb(i):
        return jnp.clip(i * NA_QROWS - NA_WIN_R // 2, 0, rows - NA_KROWS) * GRID_W // kblk

    return pl.pallas_call(
        functools.partial(_na_kernel, lat=True),
        grid=(batch, NA_HEADS // NA_HPS, nblk),
        in_specs=[
            pl.BlockSpec((tq, w), lambda b, h, i: (b * nblk + i, cq + h)),
            pl.BlockSpec((ctx_len, w), lambda b, h, i: (b, ck + h)),
            pl.BlockSpec((ctx_len, w), lambda b, h, i: (b, cv + h)),
            *[pl.BlockSpec((kblk, w), functools.partial(lambda b, h, i, j, c: (b * (seq // kblk) + kwb(i) + j, c + h),
                                                        j=j, c=ck)) for j in range(NA_KBLKS)],
            *[pl.BlockSpec((kblk, w), functools.partial(lambda b, h, i, j, c: (b * (seq // kblk) + kwb(i) + j, c + h),
                                                        j=j, c=cv)) for j in range(NA_KBLKS)],
            pl.BlockSpec((1, NA_HPS, tq, NA_KROWS * GRID_W), lambda b, h, i: (pat(i), h, 0, 0)),
        ],
        out_specs=pl.BlockSpec((tq, w), lambda b, h, i: (b * nblk + i, h)),
        out_shape=jax.ShapeDtypeStruct((batch * seq, BRANCH_W), BF16),
        compiler_params=_params("parallel", "parallel", "arbitrary"),
        name="na_attn_lat",
    )(p, pc, pc, *([p] * (2 * NA_KBLKS)), bias)


def _na_ctx_call(pc, batch, ctx_len):
    return pl.pallas_call(
        functools.partial(_na_kernel, lat=False),
        grid=(batch, NA_HEADS // 2),
        in_specs=[
            pl.BlockSpec((ctx_len, LANES), lambda b, h: (b, C_NQ + h)),
            pl.BlockSpec((ctx_len, LANES), lambda b, h: (b, C_NK + h)),
            pl.BlockSpec((ctx_len, LANES), lambda b, h: (b, C_NV + h)),
        ],
        out_specs=pl.BlockSpec((ctx_len, LANES), lambda b, h: (b, h)),
        out_shape=jax.ShapeDtypeStruct((batch * ctx_len, BRANCH_W), BF16),
        compiler_params=_params("parallel", "parallel"),
        name="na_attn_ctx",
    )(pc, pc, pc)


def _merge_kernel(x_ref, mod_ref, g0, g1, g2, g3, y0, y1, y2, y3, wb_ref, bb_ref, wo_ref, lng, lnb, o_ref, *, alpha):
    merged = None
    for idx, (g_ref, y_ref) in enumerate(((g0, y0), (g1, y1), (g2, y2), (g3, y3))):
        br = jnp.dot(y_ref[...], wb_ref[idx], preferred_element_type=F32) + bb_ref[idx:idx + 1, :]
        term = _sigmoid(g_ref[...].astype(F32)) * br
        merged = term if merged is None else merged + term
    y = jnp.dot(merged.astype(BF16), wo_ref[...], preferred_element_type=F32)
    gate = mod_ref[0, 2:3, :]
    o_ref[...] = _layer_norm(alpha * x_ref[...] + gate * y, lng[...], lnb[...])


def _merge_call(x, mod, p, ys, wb, bb, wo, lng, lnb, seq, tm, alpha):
    m, d = x.shape
    tpm = seq // tm
    in_specs = ([pl.BlockSpec((tm, d), lambda i: (i, 0)),
                 pl.BlockSpec((1, 6, d), lambda i: (i // tpm, 0, 0))]
                + [pl.BlockSpec((tm, d), functools.partial(lambda i, k: (i, k), k=k)) for k in range(N_BRANCH)]
                + [pl.BlockSpec((tm, BRANCH_W), lambda i: (i, 0))] * N_BRANCH
                + [pl.BlockSpec(wb.shape, lambda i: (0, 0, 0)),
                   pl.BlockSpec(bb.shape, lambda i: (0, 0)),
                   pl.BlockSpec(wo.shape, lambda i: (0, 0)),
                   pl.BlockSpec((1, d), lambda i: (0, 0)),
                   pl.BlockSpec((1, d), lambda i: (0, 0))])
    return pl.pallas_call(
        functools.partial(_merge_kernel, alpha=alpha),
        grid=(m // tm,),
        in_specs=in_specs,
        out_specs=pl.BlockSpec((tm, d), lambda i: (i, 0)),
        out_shape=jax.ShapeDtypeStruct((m, d), F32),
        compiler_params=_params("parallel"),
        name="merge_out",
    )(x, mod, p, p, p, p, *ys, wb, bb, wo, lng.reshape(1, d), lnb.reshape(1, d))


def _ffn_kernel(x_ref, mod_ref, w1_ref, w2_ref, lng, lnb, o_ref, h_ref, acc_ref, *, alpha):
    k = pl.program_id(1)

    @pl.when(k == 0)
    def _():
        sh = mod_ref[0, 3:4, :]
        sc = mod_ref[0, 4:5, :]
        h_ref[...] = (x_ref[...] * (1.0 + sc) + sh).astype(BF16)
        acc_ref[...] = jnp.zeros(acc_ref.shape, F32)

    a = jnp.maximum(jnp.dot(h_ref[...], w1_ref[...], preferred_element_type=F32), 0.0)
    acc_ref[...] += jnp.dot((a * a).astype(BF16), w2_ref[...], preferred_element_type=F32)

    @pl.when(k == pl.num_programs(1) - 1)
    def _():
        gate = mod_ref[0, 5:6, :]
        o_ref[...] = _layer_norm(alpha * x_ref[...] + gate * acc_ref[...], lng[...], lnb[...])


def _ffn_call(x, mod, w1, w2, lng, lnb, seq, tm, kf, alpha):
    m, d = x.shape
    tpm = seq // tm
    return pl.pallas_call(
        functools.partial(_ffn_kernel, alpha=alpha),
        grid=(m // tm, D_FF // kf),
        in_specs=[
            pl.BlockSpec((tm, d), lambda i, k: (i, 0)),
            pl.BlockSpec((1, 6, d), lambda i, k: (i // tpm, 0, 0)),
            pl.BlockSpec((d, kf), lambda i, k: (0, k)),
            pl.BlockSpec((kf, d), lambda i, k: (k, 0)),
            pl.BlockSpec((1, d), lambda i, k: (0, 0)),
            pl.BlockSpec((1, d), lambda i, k: (0, 0)),
        ],
        out_specs=pl.BlockSpec((tm, d), lambda i, k: (i, 0)),
        out_shape=jax.ShapeDtypeStruct((m, d), F32),
        scratch_shapes=[pltpu.VMEM((tm, d), BF16), pltpu.VMEM((tm, d), F32)],
        compiler_params=_params("parallel", "arbitrary"),
        name="ffn",
    )(x, mod, w1, w2, lng.reshape(1, d), lnb.reshape(1, d))


def _rope_tables(seq):
    nf = DIFF_D // 4
    t = jnp.arange(seq)
    freqs = jnp.power(ROPE_BASE, -jnp.arange(nf, dtype=F32) / nf)
    pos = jnp.stack([t // GRID_W, t % GRID_W], axis=-1).astype(F32)
    ang = pos[:, :, None] * freqs
    cos, sin = jnp.cos(ang), jnp.sin(ang)
    shape = (seq, LANES // DIFF_D, 2, 2, nf)
    cos_t = jnp.broadcast_to(cos[:, None, :, None, :], shape)
    sin_t = jnp.broadcast_to(sin[:, None, :, None, :], shape)
    half = jnp.arange(2)[None, None, None, :, None]
    sin_m = jnp.where(half == 0, -sin_t, 0.0)
    sin_p = jnp.where(half == 1, sin_t, 0.0)
    return tuple(a.reshape(seq, LANES) for a in (cos_t, sin_m, sin_p))


def kernel(x, c, ctx, c_ctx, w_ada, b_ada, w_in, b_in, conv_a_w, conv_a_b, conf_dw_w, conf_dw_b, conf_ln_g, conf_ln_b, diff_lambda, diff_norm_g, na_rpb, w_branch, b_branch, w_o, ln_g, ln_b, w_ff1, w_ff2):
    batch, seq, d = x.shape
    ctx_len = ctx.shape[1]
    depth = w_ada.shape[0]
    rows = seq // GRID_W
    assert d == D_MODEL and seq % (NA_QROWS * GRID_W) == 0 and rows >= NA_KROWS
    alpha = (2 * depth) ** 0.25

    cond_rows = -(-(batch + 1) // 8) * 8
    cond = jnp.zeros((cond_rows, d), F32).at[:batch].set(c).at[batch].set(c_ctx)
    mods = _ada_call(cond, w_ada, b_ada)

    w_in_p = jnp.concatenate([w_in[:, :, MAIN_W:], w_in[:, :, :MAIN_W]], axis=-1).astype(BF16)
    b_in_p = jnp.concatenate([b_in[:, MAIN_W:], b_in[:, :MAIN_W]], axis=-1).reshape(depth, 1, PROJ_W)
    w_branch_b = w_branch.astype(BF16)
    w_o_b = w_o.astype(BF16)
    w_ff1_b = w_ff1.astype(BF16)
    w_ff2_b = w_ff2.astype(BF16)
    rope_tabs = _rope_tables(seq)

    tm_lat = min(2048, seq)
    tq = min(512, seq)
    tk = min(1024, seq)
    na_bias = _na_bias_table(na_rpb, rows)
    m_ctx = batch * ctx_len

    xl = x.reshape(batch * seq, d)
    xc = ctx.reshape(m_ctx, d)
    for l in range(depth):
        last = l == depth - 1
        lam_init = 0.8 - 0.6 * math.exp(-0.3 * l)
        mod_lat = mods[l, :batch].reshape(batch, 6, d)
        mod_ctx = mods[l, batch:batch + 1].reshape(1, 6, d)
        local_w = (conv_a_w[l], conv_a_b[l], conf_dw_w[l], conf_dw_b[l], conf_ln_g[l], conf_ln_b[l])

        pc = _proj_call(xc, mod_ctx, w_in_p[l], b_in_p[l], None, m_ctx, m_ctx)
        p = _proj_call(xl, mod_lat, w_in_p[l], b_in_p[l], rope_tabs, seq, tm_lat)

        y_a, y_b = _local_call(p, *local_w, seq, min(512, seq))
        y_c = _diff_call(diff_lambda[l], p, pc, p, diff_norm_g[l], lam_init, batch, seq, ctx_len, seq, tq, tk)
        y_d = _na_lat_call(p, pc, na_bias[l], batch, seq, ctx_len)
        x1 = _merge_call(xl, mod_lat, p, (y_a, y_b, y_c, y_d), w_branch_b[l], b_branch[l], w_o_b[l],
                         ln_g[l, 0], ln_b[l, 0], seq, min(512, seq), alpha)
        xl = _ffn_call(x1, mod_lat, w_ff1_b[l], w_ff2_b[l], ln_g[l, 1], ln_b[l, 1], seq, min(1024, seq), 1024, alpha)

        if not last:
            yc_a, yc_b = _local_call(pc, *local_w, ctx_len, ctx_len)
            yc_c = _diff_call(diff_lambda[l], pc, pc, None, diff_norm_g[l], lam_init, batch, ctx_len, ctx_len, 0,
                              ctx_len, tk)
            yc_d = _na_ctx_call(pc, batch, ctx_len)
            xc1 = _merge_call(xc, mod_ctx, pc, (yc_a, yc_b, yc_c, yc_d), w_branch_b[l], b_branch[l], w_o_b[l],
                              ln_g[l, 0], ln_b[l, 0], m_ctx, min(512, m_ctx), alpha)
            xc = _ffn_call(xc1, mod_ctx, w_ff1_b[l], w_ff2_b[l], ln_g[l, 1], ln_b[l, 1], m_ctx, min(1024, m_ctx),
                           512, alpha)
    return xl.reshape(batch, seq, d)
```
